```python
import math
import jax, jax.numpy as jnp
from jax import lax
import numpy as np

D_MODEL = 1024
BATCH = 8
SEQ = 2048
DEPTH = 2
DEC_BATCH = 128
DEC_SEQ = 8
PAST_LEN = 8192
PAGE_SIZE = 128

N_MIXERS = 2
N_LAYERS_A = (DEPTH + 1) // 2
N_LAYERS_B = DEPTH // 2
MLA_HEADS = 8
MLA_Q_RANK = 384
MLA_KV_RANK = 256
MLA_NOPE = 128
MLA_ROPE = 64
MLA_V = 128
MLA_SCALE = (MLA_NOPE + MLA_ROPE) ** -0.5
ROPE_THETA = 10000.0
ATTN_Q_BLK = 128
MOBA_HEADS = 8
MOBA_HEAD_DIM = D_MODEL // MOBA_HEADS
MOBA_BLOCK = 256
MOBA_TOPK = 3
MOBA_Q_BLK = 16
REL_BUCKETS = 32
REL_MAX_DIST = 128
N_GROUPS = 4
EXPERTS_PER_GROUP = 8
N_EXPERTS = N_GROUPS * EXPERTS_PER_GROUP
TOPK_IN_GROUP = 2
D_EXPERT = 384
MOE_ROWS = 2048
EPS = 1e-6

kernel_name = "hybrid_mla_moba_hmoe_adaln_step"

F32 = jnp.float32


def rms_norm(x, g):
    xf = x.astype(F32)
    y = xf * lax.rsqrt(jnp.mean(xf * xf, axis=-1, keepdims=True) + EPS)
    return (y * g.astype(F32)).astype(x.dtype)


def adaln_params(c, w_ada, b_ada):
    m = jax.nn.silu(c) @ w_ada + b_ada
    return [t[:, None, :] for t in jnp.split(m, 6, axis=-1)]


def modulate(x, g, shift, scale):
    return rms_norm(x, g) * (1 + scale) + shift


def rope(x, pos):
    half = x.shape[-1] // 2
    inv_freq = ROPE_THETA ** (-jnp.arange(half, dtype=F32) / half)
    ang = pos.astype(F32)[:, None] * inv_freq
    ang = ang.reshape(ang.shape[:1] + (1,) * (x.ndim - 3) + ang.shape[1:])
    cos, sin = jnp.cos(ang), jnp.sin(ang)
    xf = x.astype(F32)
    x1, x2 = xf[..., :half], xf[..., half:]
    return jnp.concatenate([x1 * cos - x2 * sin, x1 * sin + x2 * cos], axis=-1).astype(x.dtype)


def t5_bucket(dist):
    dist = jnp.maximum(dist, 0)
    max_exact = REL_BUCKETS // 2
    large = max_exact + (jnp.log(jnp.maximum(dist, 1).astype(F32) / max_exact)
                         / math.log(REL_MAX_DIST / max_exact) * (REL_BUCKETS - max_exact)).astype(jnp.int32)
    large = jnp.minimum(large, REL_BUCKETS - 1)
    return jnp.where(dist < max_exact, dist, large)


def rel_bias(table, dist):
    h_idx = jnp.arange(table.shape[1])[None, :, None]
    return table.astype(F32)[t5_bucket(dist), h_idx]


def mla_project(h, pos, w_dq, g_q, w_uq, w_dkv, g_kv, w_uk):
    B, S, _ = h.shape
    q = (rms_norm(h @ w_dq, g_q) @ w_uq).reshape(B, S, MLA_HEADS, MLA_NOPE + MLA_ROPE)
    q_nope, q_pe = q[..., :MLA_NOPE], rope(q[..., MLA_NOPE:], pos)
    kv = h @ w_dkv
    ckv = rms_norm(kv[..., :MLA_KV_RANK], g_kv)
    kpe = rope(kv[..., MLA_KV_RANK:], pos)
    q_lat = jnp.einsum('bshn,rhn->bshr', q_nope, w_uk)
    return q_lat, q_pe, ckv, kpe


def mla_scores(q_lat, q_pe, ckv, kpe):
    s = jnp.einsum('bqhr,bkr->bhqk', q_lat, ckv) + jnp.einsum('bqhe,bke->bhqk', q_pe, kpe)
    return s.astype(F32) * MLA_SCALE


def mla_output(o_lat, w_uv, w_o):
    B, S = o_lat.shape[:2]
    o = jnp.einsum('bshr,rhv->bshv', o_lat, w_uv).reshape(B, S, MLA_HEADS * MLA_V)
    return o @ w_o


def mla_prompt_attention(q_lat, q_pe, ckv, kpe):
    B, S, H, R = q_lat.shape
    nq = S // ATTN_Q_BLK
    to_blocks = lambda t: jnp.moveaxis(t.reshape(B, nq, ATTN_Q_BLK, *t.shape[2:]), 1, 0)
    k_pos = jnp.arange(S)

    def block(args):
        i, ql, qp = args
        q_pos = i * ATTN_Q_BLK + jnp.arange(ATTN_Q_BLK)
        s = mla_scores(ql, qp, ckv, kpe)
        s = jnp.where(k_pos[None, :] <= q_pos[:, None], s, -jnp.inf)
        p = jax.nn.softmax(s, axis=-1).astype(ckv.dtype)
        return jnp.einsum('bhqk,bkr->bqhr', p, ckv)

    o = lax.map(block, (jnp.arange(nq), to_blocks(q_lat), to_blocks(q_pe)))
    return jnp.moveaxis(o, 0, 1).reshape(B, S, H, R)


def mla_sample_attention(q_lat, q_pe, ckv, kpe, ckv_past, kpe_past):
    T = q_lat.shape[1]
    P = ckv_past.shape[1]
    s_past = mla_scores(q_lat, q_pe, ckv_past, kpe_past)
    s_new = mla_scores(q_lat, q_pe, ckv, kpe)
    s_new = jnp.where(jnp.tril(jnp.ones((T, T), dtype=bool)), s_new, -jnp.inf)
    p = jax.nn.softmax(jnp.concatenate([s_past, s_new], axis=-1), axis=-1).astype(ckv.dtype)
    return (jnp.einsum('bhqk,bkr->bqhr', p[..., :P], ckv_past)
            + jnp.einsum('bhqk,bkr->bqhr', p[..., P:], ckv))


def moba_attend_one(q, q_pos, k_own, v_own, own_pos, k_src, v_src, means, block_valid, rel_table, n_sel):
    T, H, Dh = q.shape
    h_idx = jnp.arange(H)[None, :, None]
    scale = Dh ** -0.5
    s_own = jnp.einsum('qhd,khd->qhk', q, k_own).astype(F32) * scale
    s_own = s_own + rel_bias(rel_table, q_pos[:, None, None] - own_pos[None, None, :])
    s_own = jnp.where(own_pos[None, None, :] <= q_pos[:, None, None], s_own, -jnp.inf)
    if n_sel == 0:
        p = jax.nn.softmax(s_own, axis=-1).astype(v_own.dtype)
        return jnp.einsum('qhk,khd->qhd', p, v_own)
    gate = jnp.einsum('qhd,nhd->qhn', q.astype(F32), means)
    gate = jnp.where(block_valid[None, None, :], gate, -jnp.inf)
    _, idx = lax.top_k(gate, n_sel)
    sel_valid = jnp.repeat(block_valid[idx], MOBA_BLOCK, axis=-1)
    pos = (idx[..., None] * MOBA_BLOCK + jnp.arange(MOBA_BLOCK)).reshape(T, H, n_sel * MOBA_BLOCK)
    k_sel = k_src[pos, h_idx]
    v_sel = v_src[pos, h_idx]
    s_sel = jnp.einsum('qhd,qhkd->qhk', q, k_sel).astype(F32) * scale
    s_sel = s_sel + rel_bias(rel_table, q_pos[:, None, None] - pos)
    s_sel = jnp.where(sel_valid, s_sel, -jnp.inf)
    p = jax.nn.softmax(jnp.concatenate([s_sel, s_own], axis=-1), axis=-1).astype(v_own.dtype)
    n = pos.shape[-1]
    return (jnp.einsum('qhk,qhkd->qhd', p[..., :n], v_sel)
            + jnp.einsum('qhk,khd->qhd', p[..., n:], v_own))


def moba_prompt_attention(q, k, v, rel_table):
    B, S, H, Dh = q.shape
    n_blocks = -(-S // MOBA_BLOCK)
    n_past = (S - 1) // MOBA_BLOCK
    n_sel = min(MOBA_TOPK, n_past)
    pad = n_blocks * MOBA_BLOCK - S
    k_pad = jnp.pad(k, ((0, 0), (0, pad), (0, 0), (0, 0)))
    v_pad = jnp.pad(v, ((0, 0), (0, pad), (0, 0), (0, 0)))
    means = k[:, :n_past * MOBA_BLOCK].astype(F32).reshape(B, n_past, MOBA_BLOCK, H, Dh).mean(axis=2)
    nq = S // MOBA_Q_BLK
    q_blocks = jnp.moveaxis(q.reshape(B, nq, MOBA_Q_BLK, H, Dh), 1, 0)

    def block(args):
        i, qb = args
        q_start = i * MOBA_Q_BLK
        own_blk = q_start // MOBA_BLOCK
        own_start = own_blk * MOBA_BLOCK
        q_pos = q_start + jnp.arange(MOBA_Q_BLK)
        own_pos = own_start + jnp.arange(MOBA_BLOCK)
        k_own = lax.dynamic_slice_in_dim(k_pad, own_start, MOBA_BLOCK, axis=1)
        v_own = lax.dynamic_slice_in_dim(v_pad, own_start, MOBA_BLOCK, axis=1)
        block_valid = jnp.arange(n_past) < own_blk
        f = lambda qq, ko, vo, ks, vs, mm: moba_attend_one(
            qq, q_pos, ko, vo, own_pos, ks, vs, mm, block_valid, rel_table, n_sel)
        return jax.vmap(f)(qb, k_own, v_own, k, v, means)

    o = lax.map(block, (jnp.arange(nq), q_blocks))
    return jnp.moveaxis(o, 0, 1).reshape(B, S, H, Dh)


def moba_sample_attention(q, k_new, v_new, k_cache, v_cache, layer, page_table, past_len, rel_table):
    DB, T, H, Dh = q.shape
    n_past = past_len // MOBA_BLOCK
    n_sel = min(MOBA_TOPK, n_past)
    own_start = n_past * MOBA_BLOCK
    q_pos = past_len + jnp.arange(T)
    own_pos = jnp.arange(own_start, past_len + T)
    block_valid = jnp.ones((n_past,), dtype=bool)

    def one_seq(args):
        qs, ks_new, vs_new, pt = args
        k_past = k_cache[layer, pt].reshape(past_len, H, Dh)
        v_past = v_cache[layer, pt].reshape(past_len, H, Dh)
        means = k_past[:own_start].astype(F32).reshape(n_past, MOBA_BLOCK, H, Dh).mean(axis=1)
        k_own = jnp.concatenate([k_past[own_start:], ks_new], axis=0)
        v_own = jnp.concatenate([v_past[own_start:], vs_new], axis=0)
        return moba_attend_one(qs, q_pos, k_own, v_own, own_pos, k_past, v_past, means,
                               block_valid, rel_table, n_sel)

    return lax.map(one_seq, (q, k_new, v_new, page_table))


def moe_rows(t, w_rg, w_re, w_gate, w_up, w_down):
    n = t.shape[0]
    g_prob = jax.nn.softmax((t @ w_rg).astype(F32), axis=-1)
    g_top, g_idx = lax.top_k(g_prob, 1)
    e_logits = (t @ w_re).astype(F32).reshape(n, N_GROUPS, EXPERTS_PER_GROUP)
    e_in = jnp.take_along_axis(e_logits, g_idx[:, :, None], axis=1)[:, 0]
    e_top, e_idx = lax.top_k(e_in, TOPK_IN_GROUP)
    e_w = jax.nn.softmax(e_top, axis=-1) * g_top
    expert_id = g_idx * EXPERTS_PER_GROUP + e_idx
    combine = jnp.sum(jax.nn.one_hot(expert_id, N_EXPERTS, dtype=F32) * e_w[..., None], axis=1).astype(t.dtype)
    a = jnp.einsum('nd,edf->nef', t, w_gate)
    u = jnp.einsum('nd,edf->nef', t, w_up)
    act = jax.nn.silu(a) * u * combine[:, :, None]
    return jnp.einsum('nef,efd->nd', act, w_down)


def hier_moe(h, w_rg, w_re, w_gate, w_up, w_down):
    shp = h.shape
    t = h.reshape(-1, shp[-1])
    n = t.shape[0]
    n_blk = -(-n // MOE_ROWS)
    t = jnp.pad(t, ((0, n_blk * MOE_ROWS - n), (0, 0))).reshape(n_blk, MOE_ROWS, shp[-1])
    y = lax.map(lambda rows: moe_rows(rows, w_rg, w_re, w_gate, w_up, w_down), t)
    return y.reshape(-1, shp[-1])[:n].reshape(shp)


def setup_inputs(seed: int = 0) -> dict:
    key = jax.random.key(seed)
    ks = jax.random.split(key, 40)
    nrm = lambda i, shape, scale: jax.random.normal(ks[i], shape, F32) * scale
    n_pages = PAST_LEN // PAGE_SIZE
    n_phys = (5 * DEC_BATCH * n_pages + 3) // 4
    page_table = jax.random.permutation(ks[0], n_phys)[:DEC_BATCH * n_pages].reshape(DEC_BATCH, n_pages).astype(jnp.int32)
    D = D_MODEL
    return {
        "x_prompt": nrm(1, (BATCH, SEQ, D), 1.0),
        "x_sample": nrm(2, (DEC_BATCH, DEC_SEQ, D), 1.0),
        "cache_mla_ckv": nrm(3, (N_LAYERS_A, n_phys, PAGE_SIZE, MLA_KV_RANK), 1.0),
        "cache_mla_kpe": nrm(4, (N_LAYERS_A, n_phys, PAGE_SIZE, MLA_ROPE), 1.0),
        "cache_moba_k": nrm(5, (N_LAYERS_B, n_phys, PAGE_SIZE, MOBA_HEADS, MOBA_HEAD_DIM), 1.0),
        "cache_moba_v": nrm(6, (N_LAYERS_B, n_phys, PAGE_SIZE, MOBA_HEADS, MOBA_HEAD_DIM), 1.0),
        "page_table": page_table,
        "c_prompt": nrm(7, (BATCH, D), 1.0),
        "c_sample": nrm(8, (DEC_BATCH, D), 1.0),
        "w_ada": nrm(9, (DEPTH, D, 6 * D), 0.5 * D ** -0.5),
        "b_ada": nrm(10, (DEPTH, 6 * D), 0.02),
        "g_norm_mix": 1.0 + nrm(11, (DEPTH, D), 0.02),
        "g_norm_ffn": 1.0 + nrm(12, (DEPTH, D), 0.02),
        "g_final": 1.0 + nrm(13, (D,), 0.02),
        "w_dq": nrm(14, (N_LAYERS_A, D, MLA_Q_RANK), D ** -0.5),
        "g_q": 1.0 + nrm(15, (N_LAYERS_A, MLA_Q_RANK), 0.02),
        "w_uq": nrm(16, (N_LAYERS_A, MLA_Q_RANK, MLA_HEADS * (MLA_NOPE + MLA_ROPE)), MLA_Q_RANK ** -0.5),
        "w_dkv": nrm(17, (N_LAYERS_A, D, MLA_KV_RANK + MLA_ROPE), D ** -0.5),
        "g_kv": 1.0 + nrm(18, (N_LAYERS_A, MLA_KV_RANK), 0.02),
        "w_uk": nrm(19, (N_LAYERS_A, MLA_KV_RANK, MLA_HEADS, MLA_NOPE), MLA_KV_RANK ** -0.5),
        "w_uv": nrm(20, (N_LAYERS_A, MLA_KV_RANK, MLA_HEADS, MLA_V), MLA_KV_RANK ** -0.5),
        "w_o_mla": nrm(21, (N_LAYERS_A, MLA_HEADS * MLA_V, D), (MLA_HEADS * MLA_V) ** -0.5),
        "w_qkv_moba": nrm(22, (N_LAYERS_B, D, 3 * MOBA_HEADS * MOBA_HEAD_DIM), D ** -0.5),
        "w_o_moba": nrm(23, (N_LAYERS_B, MOBA_HEADS * MOBA_HEAD_DIM, D), (MOBA_HEADS * MOBA_HEAD_DIM) ** -0.5),
        "rel_bias_table": nrm(24, (REL_BUCKETS, MOBA_HEADS), 0.3),
        "w_router_group": nrm(25, (DEPTH, D, N_GROUPS), D ** -0.5),
        "w_router_expert": nrm(26, (DEPTH, D, N_EXPERTS), D ** -0.5),
        "w_exp_gate": nrm(27, (DEPTH, N_EXPERTS, D, D_EXPERT), D ** -0.5),
        "w_exp_up": nrm(28, (DEPTH, N_EXPERTS, D, D_EXPERT), D ** -0.5),
        "w_exp_down": nrm(29, (DEPTH, N_EXPERTS, D_EXPERT, D), D_EXPERT ** -0.5),
    }


def reference(x_prompt, x_sample, cache_mla_ckv, cache_mla_kpe, cache_moba_k, cache_moba_v, page_table,
              c_prompt, c_sample, w_ada, b_ada, g_norm_mix, g_norm_ffn, g_final,
              w_dq, g_q, w_uq, w_dkv, g_kv, w_uk, w_uv, w_o_mla,
              w_qkv_moba, w_o_moba, rel_bias_table,
              w_router_group, w_router_expert, w_exp_gate, w_exp_up, w_exp_down):
    B, S, D = x_prompt.shape
    DB, T, _ = x_sample.shape
    past_len = page_table.shape[1] * cache_mla_ckv.shape[2]
    pos_p = jnp.arange(S)
    pos_s = past_len + jnp.arange(T)
    xp, xs = x_prompt, x_sample
    ckv_p_l, kpe_p_l, k_p_l, v_p_l = [], [], [], []
    ckv_s_l, kpe_s_l, k_s_l, v_s_l = [], [], [], []
    for i in range(DEPTH):
        mp = adaln_params(c_prompt, w_ada[i], b_ada[i])
        ms = adaln_params(c_sample, w_ada[i], b_ada[i])
        hp = modulate(xp, g_norm_mix[i], mp[0], mp[1])
        hs = modulate(xs, g_norm_mix[i], ms[0], ms[1])
        if i % N_MIXERS == 0:
            la = i // N_MIXERS
            proj = (w_dq[la], g_q[la], w_uq[la], w_dkv[la], g_kv[la], w_uk[la])
            qlp, qpp, ckv_p, kpe_p = mla_project(hp, pos_p, *proj)
            qls, qps, ckv_s, kpe_s = mla_project(hs, pos_s, *proj)
            ckv_past = cache_mla_ckv[la, page_table].reshape(DB, past_len, MLA_KV_RANK)
            kpe_past = cache_mla_kpe[la, page_table].reshape(DB, past_len, MLA_ROPE)
            op = mla_output(mla_prompt_attention(qlp, qpp, ckv_p, kpe_p), w_uv[la], w_o_mla[la])
            osm = mla_output(mla_sample_attention(qls, qps, ckv_s, kpe_s, ckv_past, kpe_past), w_uv[la], w_o_mla[la])
            ckv_p_l.append(ckv_p); kpe_p_l.append(kpe_p)
            ckv_s_l.append(ckv_s); kpe_s_l.append(kpe_s)
        else:
            lb = i // N_MIXERS
            qkv_p = (hp @ w_qkv_moba[lb]).reshape(B, S, 3, MOBA_HEADS, MOBA_HEAD_DIM)
            qkv_s = (hs @ w_qkv_moba[lb]).reshape(DB, T, 3, MOBA_HEADS, MOBA_HEAD_DIM)
            q_p, k_p, v_p = qkv_p[:, :, 0], qkv_p[:, :, 1], qkv_p[:, :, 2]
            q_s, k_s, v_s = qkv_s[:, :, 0], qkv_s[:, :, 1], qkv_s[:, :, 2]
            op = moba_prompt_attention(q_p, k_p, v_p, rel_bias_table).reshape(B, S, D) @ w_o_moba[lb]
            osm = moba_sample_attention(q_s, k_s, v_s, cache_moba_k, cache_moba_v, lb, page_table,
                                        past_len, rel_bias_table).reshape(DB, T, D) @ w_o_moba[lb]
            k_p_l.append(k_p); v_p_l.append(v_p)
            k_s_l.append(k_s); v_s_l.append(v_s)
        xp = xp + mp[2] * op
        xs = xs + ms[2] * osm
        moe_w = (w_router_group[i], w_router_expert[i], w_exp_gate[i], w_exp_up[i], w_exp_down[i])
        hp = modulate(xp, g_norm_ffn[i], mp[3], mp[4])
        hs = modulate(xs, g_norm_ffn[i], ms[3], ms[4])
        xp = xp + mp[5] * hier_moe(hp, *moe_w)
        xs = xs + ms[5] * hier_moe(hs, *moe_w)
    y_prompt = rms_norm(xp, g_final)
    y_sample = rms_norm(xs, g_final)
    new_ckv_prompt = jnp.stack(ckv_p_l)
    new_kpe_prompt = jnp.stack(kpe_p_l)
    new_k_prompt = jnp.stack(k_p_l)
    new_v_prompt = jnp.stack(v_p_l)
    new_ckv_sample = jnp.stack(ckv_s_l)
    new_kpe_sample = jnp.stack(kpe_s_l)
    new_k_sample = jnp.stack(k_s_l)
    new_v_sample = jnp.stack(v_s_l)
    return (y_prompt, y_sample, new_ckv_prompt, new_kpe_prompt, new_k_prompt, new_v_prompt,
            new_ckv_sample, new_kpe_sample, new_k_sample, new_v_sample)
```

```python
import functools
import math

import numpy as np
import jax
import jax.numpy as jnp
from jax import lax
from jax.experimental import pallas as pl
from jax.experimental.pallas import tpu as pltpu

F32 = jnp.float32
BF16 = jnp.bfloat16
I32 = jnp.int32

EPS = 1e-6
ROPE_THETA = 10000.0
MLA_HEADS = 8
MLA_NOPE = 128
MLA_ROPE = 64
MLA_KV_RANK = 256
MLA_V = 128
MLA_SCALE = (MLA_NOPE + MLA_ROPE) ** -0.5
MLA_QK = MLA_KV_RANK + 128
MOBA_HEADS = 8
MOBA_HEAD_DIM = 128
MOBA_BLOCK = 256
MOBA_TOPK = 3
REL_BUCKETS = 32
REL_MAX_DIST = 128
N_GROUPS = 4
EXPERTS_PER_GROUP = 8
N_EXPERTS = N_GROUPS * EXPERTS_PER_GROUP
LANES = 128
NEG = -1e30

TOK_TILE = 512
ATT_TILE = 256
MOE_TILE = 256
VMEM_LIMIT = 56 * 1024 * 1024


def _t5_thresholds():
    max_exact = REL_BUCKETS // 2
    d = np.arange(1, 4 * REL_MAX_DIST).astype(np.float32)
    t = (np.log(d / np.float32(max_exact)) / np.float32(math.log(REL_MAX_DIST / max_exact))
         * np.float32(REL_BUCKETS - max_exact)).astype(np.float32)
    b = np.where(d < max_exact, d.astype(np.int32), np.minimum(max_exact + t.astype(np.int32), REL_BUCKETS - 1))
    return [int(d[np.argmax(b >= k)]) for k in range(REL_BUCKETS)]


T5_THR = _t5_thresholds()


def _cparams(sem):
    return pltpu.CompilerParams(dimension_semantics=sem, vmem_limit_bytes=VMEM_LIMIT)


def _rms(x, g):
    return x * lax.rsqrt(jnp.mean(x * x, axis=-1, keepdims=True) + EPS) * g


def _sigmoid(x):
    return 1.0 / (1.0 + jnp.exp(-x))


def _dot(a, b):
    return jnp.dot(a, b, preferred_element_type=F32)


def _dot_nt(a, b):
    return lax.dot_general(a, b, (((1,), (1,)), ((), ())), preferred_element_type=F32)


def _dot_f32(a, b):
    return jnp.dot(a, b, preferred_element_type=F32, precision=lax.Precision.HIGHEST)


def _dot_nt_f32(a, b):
    return lax.dot_general(a, b, (((1,), (1,)), ((), ())), preferred_element_type=F32,
                           precision=lax.Precision.HIGHEST)


def _split_specs(n_pt, cols, tile):
    return [pl.BlockSpec((tile, cols), lambda i: (jnp.minimum(i, n_pt - 1), 0)),
            pl.BlockSpec((tile, cols), lambda i: (jnp.maximum(i - n_pt, 0), 0))]


def _mod_specs(layer, k, n_pt, tiles_per_seq, n_batch, d, tile):
    return [pl.BlockSpec((None, None, 1, d),
                         lambda i: (layer, jnp.minimum(i // tiles_per_seq, n_batch - 1), 0, k)),
            pl.BlockSpec((None, tile, d), lambda i: (layer, jnp.maximum(i - n_pt, 0), k))]


def _full(shape):
    nd = len(shape)
    return pl.BlockSpec(shape, lambda *_: (0,) * nd)


def _ada_kernel(c_ref, w_ref, b_ref, o_ref):
    c = c_ref[...]
    s = c * _sigmoid(c)
    o_ref[...] = _dot(s.astype(BF16), w_ref[...].astype(BF16)) + b_ref[...]


def _ada(c_all, w_ada, b_ada):
    depth, d, d6 = w_ada.shape
    n = c_all.shape[0]
    cols = 1536
    return pl.pallas_call(
        _ada_kernel,
        grid=(depth, d6 // cols),
        in_specs=[pl.BlockSpec((n, d), lambda l, j: (0, 0)),
                  pl.BlockSpec((None, d, cols), lambda l, j: (l, 0, j)),
                  pl.BlockSpec((None, 1, cols), lambda l, j: (l, 0, j))],
        out_specs=pl.BlockSpec((None, n, cols), lambda l, j: (l, 0, j)),
        out_shape=jax.ShapeDtypeStruct((depth, n, d6), F32),
        compiler_params=_cparams(("arbitrary", "arbitrary")),
        name="ada_params",
    )(c_all, w_ada, b_ada.reshape(depth, 1, d6))


def _mla_proj_kernel(n_pt, xp_ref, xs_ref, shp_ref, shs_ref, scp_ref, scs_ref, gmix_ref, cos_ref, sin_ref,
                     wdq_ref, gq_ref, wuqn_ref, wuqp_ref, wuqr_ref, wuk_ref, wdkv_ref, wdkvr_ref, gkv_ref,
                     ckvp_ref, ckvs_ref, kpep_ref, kpes_ref, q_ref, k_ref, qs_ref, ks_ref):
    i = pl.program_id(0)
    is_s = i >= n_pt
    x = jnp.where(is_s, xs_ref[...], xp_ref[...])
    shift = jnp.where(is_s, shs_ref[...], shp_ref[...])
    scale = jnp.where(is_s, scs_ref[...], scp_ref[...])
    h = (_rms(x, gmix_ref[...]) * (1.0 + scale) + shift).astype(BF16)
    cos = cos_ref[...]
    sin = sin_ref[...]

    cq = _dot(h, wdq_ref[...])
    cqn = _rms(cq, gq_ref[...]).astype(BF16)
    qn = _dot(cqn, wuqn_ref[...]).astype(BF16)
    qpe = _dot(cqn, wuqp_ref[...])
    qrot = _dot(cqn, wuqr_ref[...])
    kv = _dot(h, wdkv_ref[...])
    kvrot = _dot(h, wdkvr_ref[...])
    ckv = _rms(kv[:, :MLA_KV_RANK], gkv_ref[...])
    kpe = kv[:, MLA_KV_RANK:] * cos + kvrot * sin

    pieces = []
    for hh in range(MLA_HEADS):
        sl = slice(hh * LANES, (hh + 1) * LANES)
        pieces.append(_dot(qn[:, sl], wuk_ref[hh]))
        pieces.append(qpe[:, sl] * cos + qrot[:, sl] * sin)
    q = jnp.concatenate(pieces, axis=-1)
    k = jnp.concatenate([ckv, kpe], axis=-1)
    q_ref[...] = q.astype(BF16)
    k_ref[...] = k.astype(BF16)

    @pl.when(jnp.logical_not(is_s))
    def _():
        ckvp_ref[...] = ckv
        kpep_ref[...] = kpe[:, :MLA_ROPE]

    @pl.when(is_s)
    def _():
        ckvs_ref[...] = ckv
        kpes_ref[...] = kpe[:, :MLA_ROPE]
        qs_ref[...] = q
        ks_ref[...] = k


def _rot_cols(w):
    half = MLA_ROPE // 2
    return jnp.concatenate([-w[..., half:], w[..., :half]], axis=-1)


def _mla_proj(xp, xs, mp4, ms3, layer, g_mix, cos_all, sin_all, w_dq, g_q, w_uq, w_dkv, g_kv, w_uk,
              seq_len, tile):
    n_p, d = xp.shape
    n_s = xs.shape[0]
    n_pt, n_st = n_p // tile, n_s // tile
    tps = seq_len // tile
    n_batch = n_p // seq_len
    qr = w_dq.shape[1]
    hq = MLA_HEADS * LANES
    wq = w_uq.reshape(qr, MLA_HEADS, MLA_NOPE + MLA_ROPE)
    w_n = wq[:, :, :MLA_NOPE].reshape(qr, hq).astype(BF16)
    zpad = jnp.zeros((qr, MLA_HEADS, LANES - MLA_ROPE), F32)
    w_p = jnp.concatenate([wq[:, :, MLA_NOPE:], zpad], axis=-1).reshape(qr, hq).astype(BF16)
    w_r = jnp.concatenate([_rot_cols(wq[:, :, MLA_NOPE:]), zpad], axis=-1).reshape(qr, hq).astype(BF16)
    w_ukt = jnp.transpose(w_uk, (1, 2, 0)).astype(BF16)
    zk = jnp.zeros((d, LANES - MLA_ROPE), F32)
    w_kv = jnp.concatenate([w_dkv, zk], axis=-1).astype(BF16)
    w_kvr = jnp.concatenate([_rot_cols(w_dkv[:, MLA_KV_RANK:]), zk], axis=-1).astype(BF16)
    n_tab = cos_all.shape[0] // tile - 1

    tab_spec = pl.BlockSpec((tile, LANES), lambda i: (jnp.where(i < n_pt, i % tps, n_tab), 0))
    in_specs = (_split_specs(n_pt, d, tile)
                + _mod_specs(layer, 0, n_pt, tps, n_batch, d, tile)
                + _mod_specs(layer, 1, n_pt, tps, n_batch, d, tile)
                + [_full((1, d)), tab_spec, tab_spec,
                   _full((d, qr)), _full((1, qr)), _full((qr, hq)), _full((qr, hq)), _full((qr, hq)),
                   _full((MLA_HEADS, MLA_NOPE, MLA_KV_RANK)), _full((d, MLA_QK)), _full((d, LANES)),
                   _full((1, MLA_KV_RANK))])
    n_all = n_p + n_s
    out_shape = (jax.ShapeDtypeStruct((n_p, MLA_KV_RANK), F32), jax.ShapeDtypeStruct((n_s, MLA_KV_RANK), F32),
                 jax.ShapeDtypeStruct((n_p, MLA_ROPE), F32), jax.ShapeDtypeStruct((n_s, MLA_ROPE), F32),
                 jax.ShapeDtypeStruct((n_all, MLA_HEADS * MLA_QK), BF16),
                 jax.ShapeDtypeStruct((n_all, MLA_QK), BF16),
                 jax.ShapeDtypeStruct((n_s, MLA_HEADS * MLA_QK), F32),
                 jax.ShapeDtypeStruct((n_s, MLA_QK), F32))
    out_specs = (_split_specs(n_pt, MLA_KV_RANK, tile) + _split_specs(n_pt, MLA_ROPE, tile)
                 + [pl.BlockSpec((tile, MLA_HEADS * MLA_QK), lambda i: (i, 0)),
                    pl.BlockSpec((tile, MLA_QK), lambda i: (i, 0)),
                    _split_specs(n_pt, MLA_HEADS * MLA_QK, tile)[1],
                    _split_specs(n_pt, MLA_QK, tile)[1]])
    return pl.pallas_call(
        functools.partial(_mla_proj_kernel, n_pt),
        grid=(n_pt + n_st,),
        in_specs=in_specs, out_specs=out_specs, out_shape=out_shape,
        compiler_params=_cparams(("arbitrary",)),
        name="mla_proj",
    )(xp, xs, mp4, ms3, mp4, ms3, g_mix.reshape(1, d), cos_all, sin_all,
      w_dq.astype(BF16), g_q.reshape(1, qr), w_n, w_p, w_r, w_ukt, w_kv, w_kvr, g_kv.reshape(1, MLA_KV_RANK))


def _mla_prompt_kernel(tq, q_ref, k_ref, o_ref, acc_ref):
    i = pl.program_id(1)
    m_rows = MLA_HEADS * tq
    q = q_ref[...]
    qs = jnp.concatenate([q[:, h * MLA_QK:(h + 1) * MLA_QK] for h in range(MLA_HEADS)], axis=0)

    def step(kblk, m, l, masked):
        s = _dot_nt(qs, kblk) * MLA_SCALE
        if masked:
            r = lax.broadcasted_iota(I32, s.shape, 0) & (tq - 1)
            c = lax.broadcasted_iota(I32, s.shape, 1)
            s = jnp.where(c <= r, s, NEG)
        m_new = jnp.maximum(m, jnp.max(s, axis=-1, keepdims=True))
        alpha = jnp.exp(m - m_new)
        p = jnp.exp(s - m_new)
        l_new = alpha * l + jnp.sum(p, axis=-1, keepdims=True)
        pv = _dot(p.astype(BF16), kblk[:, :MLA_KV_RANK])
        return m_new, l_new, alpha, pv

    kd = k_ref[pl.ds(pl.multiple_of(i * tq, tq), tq), :]
    m0 = jnp.full((m_rows, 1), NEG, F32)
    l0 = jnp.zeros((m_rows, 1), F32)
    m, l, _, pv = step(kd, m0, l0, True)
    acc_ref[...] = pv

    def body(j, carry):
        m, l = carry
        kb = k_ref[pl.ds(pl.multiple_of(j * tq, tq), tq), :]
        m, l, alpha, pv = step(kb, m, l, False)
        acc_ref[...] = acc_ref[...] * alpha + pv
        return m, l

    m, l = lax.fori_loop(0, i, body, (m, l))
    o = acc_ref[...] / l
    for h in range(MLA_HEADS):
        o_ref[:, h * MLA_KV_RANK:(h + 1) * MLA_KV_RANK] = o[h * tq:(h + 1) * tq, :].astype(o_ref.dtype)


def _mla_prompt_attn(q_all, k_all, n_batch, seq_len, tq):
    nq = seq_len // tq
    return pl.pallas_call(
        functools.partial(_mla_prompt_kernel, tq),
        grid=(n_batch, nq),
        in_specs=[pl.BlockSpec((tq, MLA_HEADS * MLA_QK), lambda b, i: (b * nq + i, 0)),
                  pl.BlockSpec((seq_len, MLA_QK), lambda b, i: (b, 0))],
        out_specs=pl.BlockSpec((tq, MLA_HEADS * MLA_KV_RANK), lambda b, i: (b * nq + i, 0)),
        out_shape=jax.ShapeDtypeStruct((n_batch * seq_len, MLA_HEADS * MLA_KV_RANK), BF16),
        scratch_shapes=[pltpu.VMEM((MLA_HEADS * tq, MLA_KV_RANK), F32)],
        compiler_params=_cparams(("arbitrary", "arbitrary")),
        name="mla_prompt_attn",
    )(q_all, k_all)


def _mla_sample_kernel(layer, n_pages, page, t_new, chunk,
                       pt_ref, q_ref, k_ref, ckv_hbm, kpe_hbm, o_ref,
                       ckv_buf, kpe_buf, kb_buf, s_buf, sem):
    b = pl.program_id(0)
    nb = pl.num_programs(0)
    slot = b % 2
    past = n_pages * page

    def copies(seq, sl):
        out = []
        for j in range(n_pages):
            pid = pt_ref[seq, j]
            out.append(pltpu.make_async_copy(ckv_hbm.at[layer, pid], ckv_buf.at[sl, pl.ds(j * page, page)], sem.at[0, sl]))
            out.append(pltpu.make_async_copy(kpe_hbm.at[layer, pid], kpe_buf.at[sl, pl.ds(j * page, page)], sem.at[1, sl]))
        return out

    @pl.when(b == 0)
    def _():
        for cp in copies(0, 0):
            cp.start()

    @pl.when(b + 1 < nb)
    def _():
        for cp in copies(b + 1, 1 - slot):
            cp.start()

    for cp in copies(b, slot):
        cp.wait()

    q = q_ref[...]
    qs = jnp.concatenate([q[:, h * MLA_QK:(h + 1) * MLA_QK] for h in range(MLA_HEADS)], axis=0).astype(BF16)
    ql = qs[:, :MLA_KV_RANK]
    qp = qs[:, MLA_KV_RANK:MLA_KV_RANK + MLA_ROPE]
    rows = MLA_HEADS * t_new
    for c in range(past // chunk):
        kc = ckv_buf[slot, c * chunk:(c + 1) * chunk, :].astype(BF16)
        pc = kpe_buf[slot, c * chunk:(c + 1) * chunk, :].astype(BF16)
        kb_buf[c * chunk:(c + 1) * chunk, :] = kc
        s_buf[:, c * chunk:(c + 1) * chunk] = (_dot_nt(ql, kc) + _dot_nt(qp, pc)) * MLA_SCALE

    kn = k_ref[...].astype(BF16)
    s_new = _dot_nt(qs, kn) * MLA_SCALE
    r = lax.broadcasted_iota(I32, s_new.shape, 0) & (t_new - 1)
    c_ = lax.broadcasted_iota(I32, s_new.shape, 1)
    s_new = jnp.where(c_ <= r, s_new, NEG)
    s = s_buf[...]
    m = jnp.maximum(jnp.max(s, axis=-1, keepdims=True), jnp.max(s_new, axis=-1, keepdims=True))
    p_new = jnp.exp(s_new - m)
    l = jnp.sum(p_new, axis=-1, keepdims=True)
    acc = _dot(p_new.astype(BF16), kn[:, :MLA_KV_RANK])
    for c in range(past // chunk):
        p = jnp.exp(s_buf[:, c * chunk:(c + 1) * chunk] - m)
        l = l + jnp.sum(p, axis=-1, keepdims=True)
        acc = acc + _dot(p.astype(BF16), kb_buf[c * chunk:(c + 1) * chunk, :])
    o = acc / l
    for h in range(MLA_HEADS):
        o_ref[:, h * MLA_KV_RANK:(h + 1) * MLA_KV_RANK] = o[h * t_new:(h + 1) * t_new, :]


def _mla_sample_attn(page_table, q_s, k_s, cache_ckv, cache_kpe, layer, t_new):
    n_seq, n_pages = page_table.shape
    page = cache_ckv.shape[2]
    past = n_pages * page
    chunk = min(1024, past)
    rows = MLA_HEADS * t_new
    grid_spec = pltpu.PrefetchScalarGridSpec(
        num_scalar_prefetch=1,
        grid=(n_seq,),
        in_specs=[pl.BlockSpec((t_new, MLA_HEADS * MLA_QK), lambda b, pt: (b, 0)),
                  pl.BlockSpec((t_new, MLA_QK), lambda b, pt: (b, 0)),
                  pl.BlockSpec(memory_space=pl.ANY),
                  pl.BlockSpec(memory_space=pl.ANY)],
        out_specs=pl.BlockSpec((t_new, MLA_HEADS * MLA_KV_RANK), lambda b, pt: (b, 0)),
        scratch_shapes=[pltpu.VMEM((2, past, MLA_KV_RANK), F32),
                        pltpu.VMEM((2, past, MLA_ROPE), F32),
                        pltpu.VMEM((past, MLA_KV_RANK), BF16),
                        pltpu.VMEM((rows, past), F32),
                        pltpu.SemaphoreType.DMA((2, 2))])
    return pl.pallas_call(
        functools.partial(_mla_sample_kernel, layer, n_pages, page, t_new, chunk),
        grid_spec=grid_spec,
        out_shape=jax.ShapeDtypeStruct((n_seq * t_new, MLA_HEADS * MLA_KV_RANK), F32),
        compiler_params=_cparams(("arbitrary",)),
        name="mla_sample_attn",
    )(page_table, q_s, k_s, cache_ckv, cache_kpe)


def _route(z):
    lane = lax.broadcasted_iota(I32, z.shape, 1)
    big = jnp.int32(1 << 20)
    gmask = lane < N_GROUPS
    zg = jnp.where(gmask, z, NEG)
    gmax = jnp.max(zg, axis=-1, keepdims=True)
    gidx = jnp.min(jnp.where(jnp.logical_and(gmask, zg == gmax), lane, big), axis=-1, keepdims=True)
    gsum = jnp.sum(jnp.where(gmask, jnp.exp(zg - gmax), 0.0), axis=-1, keepdims=True)
    g_top = 1.0 / gsum
    lo = N_GROUPS + gidx * EXPERTS_PER_GROUP
    emask = jnp.logical_and(lane >= lo, lane < lo + EXPERTS_PER_GROUP)
    z1 = jnp.where(emask, z, NEG)
    e1 = jnp.max(z1, axis=-1, keepdims=True)
    i1 = jnp.min(jnp.where(jnp.logical_and(emask, z1 == e1), lane, big), axis=-1, keepdims=True)
    z2 = jnp.where(lane == i1, NEG, z1)
    e2 = jnp.max(z2, axis=-1, keepdims=True)
    i2 = jnp.min(jnp.where(jnp.logical_and(emask, jnp.logical_and(z2 == e2, lane != i1)), lane, big),
                 axis=-1, keepdims=True)
    t = jnp.exp(e2 - e1)
    w1 = g_top / (1.0 + t)
    w2 = g_top * t / (1.0 + t)
    out = jnp.where(lane == 0, (i1 - N_GROUPS).astype(F32),
                    jnp.where(lane == 1, (i2 - N_GROUPS).astype(F32),
                              jnp.where(lane == 2, w1, jnp.where(lane == 3, w2, 0.0))))
    return out


def _attn_out_kernel(n_pt, split_x, use_uv, *refs):
    refs = list(refs)
    op_ref, os_ref = refs[:2]
    refs = refs[2:]
    if split_x:
        xp_ref, xs_ref = refs[:2]
        refs = refs[2:]
    else:
        x_ref = refs[0]
        refs = refs[1:]
    (gp_ref, gs_ref, shp_ref, shs_ref, scp_ref, scs_ref, gffn_ref) = refs[:7]
    refs = refs[7:]
    if use_uv:
        wuv_ref = refs[0]
        refs = refs[1:]
    wo_ref, wr_ref, xo_ref, h2_ref, rt_ref = refs

    i = pl.program_id(0)
    is_s = i >= n_pt
    o = jnp.where(is_s, os_ref[...].astype(BF16), op_ref[...])
    if split_x:
        x = jnp.where(is_s, xs_ref[...], xp_ref[...])
    else:
        x = x_ref[...]
    gate = jnp.where(is_s, gs_ref[...], gp_ref[...])
    shift = jnp.where(is_s, shs_ref[...], shp_ref[...])
    scale = jnp.where(is_s, scs_ref[...], scp_ref[...])
    if use_uv:
        o = jnp.concatenate(
            [_dot(o[:, h * MLA_KV_RANK:(h + 1) * MLA_KV_RANK], wuv_ref[h]) for h in range(MLA_HEADS)],
            axis=-1).astype(BF16)
    a = _dot(o, wo_ref[...])
    xn = x + gate * a
    xo_ref[...] = xn
    h2 = _rms(xn, gffn_ref[...]) * (1.0 + scale) + shift
    h2_ref[...] = h2
    rt_ref[...] = _route(_dot_f32(h2, wr_ref[...]))


def _attn_out(o_p, o_s, x, mp4, ms3, layer, g_ffn, w_uv, w_o, w_rg, w_re, seq_len, tile):
    split_x = isinstance(x, tuple)
    n_p = o_p.shape[0]
    n_s = o_s.shape[0]
    din = o_p.shape[1]
    d = w_o.shape[1]
    n_pt, n_st = n_p // tile, n_s // tile
    tps = seq_len // tile
    n_batch = n_p // seq_len
    use_uv = w_uv is not None
    wr = jnp.concatenate([w_rg, w_re, jnp.zeros((d, LANES - N_GROUPS - N_EXPERTS), F32)], axis=-1)
    in_specs = _split_specs(n_pt, din, tile)
    args = [o_p, o_s]
    if split_x:
        in_specs += _split_specs(n_pt, d, tile)
        args += list(x)
    else:
        in_specs += [pl.BlockSpec((tile, d), lambda i: (i, 0))]
        args += [x]
    for k in (2, 3, 4):
        in_specs += _mod_specs(layer, k, n_pt, tps, n_batch, d, tile)
        args += [mp4, ms3]
    in_specs += [_full((1, d))]
    args += [g_ffn.reshape(1, d)]
    if use_uv:
        in_specs += [_full((MLA_HEADS, MLA_KV_RANK, MLA_V))]
        args += [jnp.transpose(w_uv, (1, 0, 2)).astype(BF16)]
    in_specs += [_full(w_o.shape), _full((d, LANES))]
    args += [w_o.astype(BF16), wr]
    n_all = n_p + n_s
    tok = lambda cols: pl.BlockSpec((tile, cols), lambda i: (i, 0))
    return pl.pallas_call(
        functools.partial(_attn_out_kernel, n_pt, split_x, use_uv),
        grid=(n_pt + n_st,),
        in_specs=in_specs,
        out_specs=[tok(d), tok(d), tok(LANES)],
        out_shape=(jax.ShapeDtypeStruct((n_all, d), F32), jax.ShapeDtypeStruct((n_all, d), F32),
                   jax.ShapeDtypeStruct((n_all, LANES), F32)),
        compiler_params=_cparams(("arbitrary",)),
        name="attn_out_router",
    )(*args)


def _moe_kernel(tm, te_ref, tr_ref, src_ref, srcn_ref, dst_ref, w_ref, h_hbm, wg_ref, wu_ref, wd_ref,
                y_hbm, xbuf, ybuf, gsem, ssem):
    t = pl.program_id(0)
    nt = pl.num_programs(0)
    slot = t % 2
    rows = tr_ref[t]

    def gather(idx_ref, n, sl):
        def body(r, carry):
            tok = idx_ref[0, 0, r]
            pltpu.make_async_copy(h_hbm.at[pl.ds(tok, 1)], xbuf.at[sl, pl.ds(r, 1)], gsem.at[sl]).start()
            return carry
        lax.fori_loop(0, n, body, 0)

    def wait_rows(n, make):
        def body(r, carry):
            make().wait()
            return carry
        lax.fori_loop(0, n, body, 0)

    @pl.when(t == 0)
    def _():
        xbuf[...] = jnp.zeros(xbuf.shape, xbuf.dtype)
        gather(src_ref, rows, 0)

    @pl.when(t + 1 < nt)
    def _():
        gather(srcn_ref, tr_ref[jnp.minimum(t + 1, nt - 1)], 1 - slot)

    wait_rows(rows, lambda: pltpu.make_async_copy(h_hbm.at[pl.ds(0, 1)], xbuf.at[slot, pl.ds(0, 1)], gsem.at[slot]))

    scatter_wait = lambda: pltpu.make_async_copy(ybuf.at[pl.ds(0, 1)], y_hbm.at[pl.ds(0, 1)], ssem.at[0])

    @pl.when(t > 0)
    def _():
        wait_rows(tr_ref[jnp.maximum(t - 1, 0)], scatter_wait)

    @pl.when(rows > 0)
    def _():
        xb = xbuf[slot].astype(BF16)
        a = _dot(xb, wg_ref[...].astype(BF16))
        u = _dot(xb, wu_ref[...].astype(BF16))
        act = (a * _sigmoid(a)) * u * w_ref[...]
        ybuf[...] = _dot(act.astype(BF16), wd_ref[...].astype(BF16))

        def body(r, carry):
            drow = dst_ref[0, 0, r]
            pltpu.make_async_copy(ybuf.at[pl.ds(r, 1)], y_hbm.at[pl.ds(drow, 1)], ssem.at[0]).start()
            return carry
        lax.fori_loop(0, rows, body, 0)

    @pl.when(t == nt - 1)
    def _():
        wait_rows(rows, scatter_wait)


def _moe(h2, route, w_gate, w_up, w_down, tm):
    n, d = h2.shape
    n_exp, _, f = w_gate.shape
    eid = route[:, :2].astype(I32)
    ew = route[:, 2:4]
    n2 = 2 * n
    flat = eid.reshape(n2)
    order = jnp.argsort(flat, stable=True).astype(I32)
    counts = jnp.sum((flat[:, None] == jnp.arange(n_exp, dtype=I32)[None, :]).astype(I32), axis=0)
    padded = ((counts + tm - 1) // tm) * tm
    pend = jnp.cumsum(padded)
    pstart = pend - padded
    ustart = jnp.cumsum(counts) - counts
    n_tiles = n2 // tm + n_exp
    tstart = jnp.arange(n_tiles, dtype=I32) * tm
    te = jnp.minimum(jnp.searchsorted(pend, tstart, side="right").astype(I32), n_exp - 1)
    r0 = tstart - pstart[te]
    tr = jnp.clip(counts[te] - r0, 0, tm).astype(I32)
    pos = (ustart[te] + r0)[:, None] + jnp.arange(tm, dtype=I32)[None, :]
    fidx = order[jnp.clip(pos, 0, n2 - 1)]
    src = (fidx // 2).reshape(n_tiles, 1, tm)
    dst = fidx.reshape(n_tiles, 1, tm)
    wslot = ew.reshape(n2)[fidx].reshape(n_tiles, tm, 1)

    smem_blk = lambda f_: pl.BlockSpec((1, 1, tm), f_, memory_space=pltpu.SMEM)
    grid_spec = pltpu.PrefetchScalarGridSpec(
        num_scalar_prefetch=2,
        grid=(n_tiles,),
        in_specs=[smem_blk(lambda t, te_, tr_: (t, 0, 0)),
                  smem_blk(lambda t, te_, tr_: (jnp.minimum(t + 1, n_tiles - 1), 0, 0)),
                  smem_blk(lambda t, te_, tr_: (t, 0, 0)),
                  pl.BlockSpec((None, tm, 1), lambda t, te_, tr_: (t, 0, 0)),
                  pl.BlockSpec(memory_space=pl.ANY),
                  pl.BlockSpec((None, d, f), lambda t, te_, tr_: (te_[t], 0, 0)),
                  pl.BlockSpec((None, d, f), lambda t, te_, tr_: (te_[t], 0, 0)),
                  pl.BlockSpec((None, f, d), lambda t, te_, tr_: (te_[t], 0, 0))],
        out_specs=pl.BlockSpec(memory_space=pl.ANY),
        scratch_shapes=[pltpu.VMEM((2, tm, d), F32), pltpu.VMEM((tm, d), F32),
                        pltpu.SemaphoreType.DMA((2,)), pltpu.SemaphoreType.DMA((1,))])
    y2 = pl.pallas_call(
        functools.partial(_moe_kernel, tm),
        grid_spec=grid_spec,
        out_shape=jax.ShapeDtypeStruct((n2, d), F32),
        compiler_params=_cparams(("arbitrary",)),
        name="moe_experts",
    )(te, tr, src, src, dst, wslot, h2, w_gate, w_up, w_down)
    return y2.reshape(n, 2 * d)


def _combine_kernel(n_pt, final, d, *refs):
    if final:
        x_ref, y_ref, gp_ref, gs_ref, gf_ref, op_ref, os_ref = refs
    else:
        x_ref, y_ref, gp_ref, gs_ref, o_ref = refs
    i = pl.program_id(0)
    is_s = i >= n_pt
    gate = jnp.where(is_s, gs_ref[...], gp_ref[...])
    y = y_ref[...]
    xn = x_ref[...] + gate * (y[:, :d] + y[:, d:])
    if final:
        out = _rms(xn, gf_ref[...])

        @pl.when(jnp.logical_not(is_s))
        def _():
            op_ref[...] = out

        @pl.when(is_s)
        def _():
            os_ref[...] = out
    else:
        o_ref[...] = xn


def _combine(x, y2, mp4, ms3, layer, g_final, n_p, seq_len, tile):
    n_all, d = x.shape
    n_s = n_all - n_p
    n_pt, n_st = n_p // tile, n_s // tile
    tps = seq_len // tile
    n_batch = n_p // seq_len
    final = g_final is not None
    tok = lambda cols: pl.BlockSpec((tile, cols), lambda i: (i, 0))
    in_specs = [tok(d), tok(2 * d)] + _mod_specs(layer, 5, n_pt, tps, n_batch, d, tile)
    args = [x, y2, mp4, ms3]
    if final:
        in_specs += [_full((1, d))]
        args += [g_final.reshape(1, d)]
        out_specs = _split_specs(n_pt, d, tile)
        out_shape = (jax.ShapeDtypeStruct((n_p, d), F32), jax.ShapeDtypeStruct((n_s, d), F32))
    else:
        out_specs = tok(d)
        out_shape = jax.ShapeDtypeStruct((n_all, d), F32)
    return pl.pallas_call(
        functools.partial(_combine_kernel, n_pt, final, d),
        grid=(n_pt + n_st,),
        in_specs=in_specs, out_specs=out_specs, out_shape=out_shape,
        compiler_params=_cparams(("arbitrary",)),
        name="moe_combine",
    )(*args)


def _moba_qkv_kernel(n_pt, d, x_ref, shp_ref, shs_ref, scp_ref, scs_ref, gmix_ref, w_ref,
                     q_ref, kp_ref, ks_ref, vp_ref, vs_ref):
    i = pl.program_id(0)
    is_s = i >= n_pt
    shift = jnp.where(is_s, shs_ref[...], shp_ref[...])
    scale = jnp.where(is_s, scs_ref[...], scp_ref[...])
    h = (_rms(x_ref[...], gmix_ref[...]) * (1.0 + scale) + shift).astype(BF16)
    qkv = _dot(h, w_ref[...])
    q_ref[...] = qkv[:, :d]

    @pl.when(jnp.logical_not(is_s))
    def _():
        kp_ref[...] = qkv[:, d:2 * d]
        vp_ref[...] = qkv[:, 2 * d:]

    @pl.when(is_s)
    def _():
        ks_ref[...] = qkv[:, d:2 * d]
        vs_ref[...] = qkv[:, 2 * d:]


def _moba_qkv(x, mp4, ms3, layer, g_mix, w_qkv, n_p, seq_len, tile):
    n_all, d = x.shape
    n_s = n_all - n_p
    n_pt, n_st = n_p // tile, n_s // tile
    tps = seq_len // tile
    n_batch = n_p // seq_len
    tok = lambda cols: pl.BlockSpec((tile, cols), lambda i: (i, 0))
    in_specs = ([tok(d)] + _mod_specs(layer, 0, n_pt, tps, n_batch, d, tile)
                + _mod_specs(layer, 1, n_pt, tps, n_batch, d, tile) + [_full((1, d)), _full((d, 3 * d))])
    sp = _split_specs(n_pt, d, tile)
    return pl.pallas_call(
        functools.partial(_moba_qkv_kernel, n_pt, d),
        grid=(n_pt + n_st,),
        in_specs=in_specs,
        out_specs=[tok(d), sp[0], sp[1], sp[0], sp[1]],
        out_shape=(jax.ShapeDtypeStruct((n_all, d), F32),
                   jax.ShapeDtypeStruct((n_p, d), F32), jax.ShapeDtypeStruct((n_s, d), F32),
                   jax.ShapeDtypeStruct((n_p, d), F32), jax.ShapeDtypeStruct((n_s, d), F32)),
        compiler_params=_cparams(("arbitrary",)),
        name="moba_qkv",
    )(x, mp4, ms3, mp4, ms3, g_mix.reshape(1, d), w_qkv.astype(BF16))


def _t5_bias(dist, table_at):
    val = jnp.full(dist.shape, 0.0, F32) + table_at(0)
    for b in range(1, REL_BUCKETS):
        val = jnp.where(dist >= T5_THR[b], table_at(b), val)
    return val


def _moba_prompt_kernel(q_ref, k_ref, v_ref, tab_ref, o_ref, mean_ref, bown_ref, bprev_ref, acc_ref):
    h = pl.program_id(1)
    i = pl.program_id(2)
    blk = MOBA_BLOCK
    nblk = k_ref.shape[0] // blk
    scale = MOBA_HEAD_DIM ** -0.5

    @pl.when(i == 0)
    def _():
        kf = k_ref[...]
        for n in range(nblk):
            mean_ref[n:n + 1, :] = jnp.sum(kf[n * blk:(n + 1) * blk, :], axis=0, keepdims=True) * (1.0 / blk)
        r = lax.broadcasted_iota(I32, (blk, blk), 0)
        c = lax.broadcasted_iota(I32, (blk, blk), 1)
        at = lambda b: tab_ref[b, h]
        bown_ref[...] = _t5_bias(jnp.maximum(r - c, 0), at)
        bprev_ref[...] = _t5_bias(blk + r - c, at)

    q = q_ref[...]
    qb = q.astype(BF16)
    gate_t = _dot_nt_f32(mean_ref[...], q)
    bi = lax.broadcasted_iota(I32, gate_t.shape, 0)
    valid = bi < i
    gate_t = jnp.where(valid, gate_t, NEG)
    rank = jnp.zeros(gate_t.shape, I32)
    for m_ in range(nblk - 1):
        gm = gate_t[m_:m_ + 1, :]
        ahead = jnp.logical_or(gm > gate_t, jnp.logical_and(gm == gate_t, m_ < bi))
        rank = rank + jnp.where(jnp.logical_and(ahead, m_ < i), 1, 0)
    sel_t = jnp.where(jnp.logical_and(valid, rank < MOBA_TOPK), 1.0, 0.0)
    sel = jnp.transpose(jnp.concatenate([sel_t, jnp.zeros((LANES - nblk, blk), F32)], axis=0))

    off = pl.multiple_of(i * blk, blk)
    ko = k_ref[pl.ds(off, blk), :].astype(BF16)
    vo = v_ref[pl.ds(off, blk), :].astype(BF16)
    s = _dot_nt(qb, ko) * scale + bown_ref[...]
    r = lax.broadcasted_iota(I32, s.shape, 0)
    c = lax.broadcasted_iota(I32, s.shape, 1)
    s = jnp.where(c <= r, s, NEG)
    m = jnp.max(s, axis=-1, keepdims=True)
    p = jnp.exp(s - m)
    l = jnp.sum(p, axis=-1, keepdims=True)
    acc_ref[...] = _dot(p.astype(BF16), vo)
    far = tab_ref[REL_BUCKETS - 1, h]
    lane = lax.broadcasted_iota(I32, sel.shape, 1)

    def body(j, carry):
        m, l = carry
        offj = pl.multiple_of(j * blk, blk)
        kj = k_ref[pl.ds(offj, blk), :].astype(BF16)
        vj = v_ref[pl.ds(offj, blk), :].astype(BF16)
        bias = jnp.where(j == i - 1, bprev_ref[...], far)
        s = _dot_nt(qb, kj) * scale + bias
        selj = jnp.sum(jnp.where(lane == j, sel, 0.0), axis=-1, keepdims=True)
        s = jnp.where(selj > 0.5, s, NEG)
        m_new = jnp.maximum(m, jnp.max(s, axis=-1, keepdims=True))
        alpha = jnp.exp(m - m_new)
        p = jnp.exp(s - m_new)
        l = alpha * l + jnp.sum(p, axis=-1, keepdims=True)
        acc_ref[...] = acc_ref[...] * alpha + _dot(p.astype(BF16), vj)
        return m_new, l

    m, l = lax.fori_loop(0, i, body, (m, l))
    o_ref[...] = (acc_ref[...] / l).astype(o_ref.dtype)


def _moba_prompt_attn(q_all, k_p, v_p, rel_table, n_batch, seq_len):
    blk = MOBA_BLOCK
    nq = seq_len // blk
    dh = MOBA_HEAD_DIM
    nh = MOBA_HEADS
    return pl.pallas_call(
        _moba_prompt_kernel,
        grid=(n_batch, nh, nq),
        in_specs=[pl.BlockSpec((blk, dh), lambda b, h, i: (b * nq + i, h)),
                  pl.BlockSpec((seq_len, dh), lambda b, h, i: (b, h)),
                  pl.BlockSpec((seq_len, dh), lambda b, h, i: (b, h)),
                  pl.BlockSpec(memory_space=pltpu.SMEM)],
        out_specs=pl.BlockSpec((blk, dh), lambda b, h, i: (b * nq + i, h)),
        out_shape=jax.ShapeDtypeStruct((n_batch * seq_len, nh * dh), BF16),
        scratch_shapes=[pltpu.VMEM((nq, dh), F32), pltpu.VMEM((blk, blk), F32), pltpu.VMEM((blk, blk), F32),
                        pltpu.VMEM((blk, dh), F32)],
        compiler_params=_cparams(("arbitrary", "arbitrary", "arbitrary")),
        name="moba_prompt_attn",
    )(q_all, k_p, v_p, rel_table)


def _moba_sample_kernel(layer, n_pages, page, t_new, cpages,
                        pt_ref, q_ref, kn_ref, vn_ref, trow_ref, k_hbm, v_hbm, o_ref,
                        buf, sum_ref, s_buf, sem):
    b = pl.program_id(0)
    nb = pl.num_programs(0)
    nh, dh, blk = MOBA_HEADS, MOBA_HEAD_DIM, MOBA_BLOCK
    rows = nh * t_new
    past = n_pages * page
    chunk = cpages * page
    prow = page * nh
    n_chunks = n_pages // cpages
    nblk = past // blk
    bpc = chunk // blk
    scale = dh ** -0.5

    def copies(seq, ci, src, sl):
        return [pltpu.make_async_copy(src.at[layer, pt_ref[seq, ci * cpages + j]],
                                      buf.at[sl, pl.ds(j * prow, prow)], sem.at[sl])
                for j in range(cpages)]

    @pl.when(b == 0)
    def _():
        for cp in copies(0, 0, k_hbm, 0):
            cp.start()

    q = q_ref[...]
    qh = [q[:, h * dh:(h + 1) * dh] for h in range(nh)]
    qh_b = [x.astype(BF16) for x in qh]

    for ci in range(n_chunks):
        sl = ci % 2
        nxt = copies(b, ci + 1, k_hbm, 1 - sl) if ci + 1 < n_chunks else copies(b, 0, v_hbm, 1 - sl)
        for cp in nxt:
            cp.start()
        for cp in copies(b, ci, k_hbm, sl):
            cp.wait()
        for n in range(bpc):
            kblk = buf[sl, n * blk * nh:(n + 1) * blk * nh, :]
            sum_ref[ci * bpc + n] = jnp.sum(kblk.reshape(blk, nh, dh), axis=0)
        for h in range(nh):
            kh = buf[sl, pl.ds(h, chunk, stride=nh), :].astype(BF16)
            s_buf[h * t_new:(h + 1) * t_new, ci * chunk:(ci + 1) * chunk] = _dot_nt(qh_b[h], kh) * scale

    gate = jnp.concatenate(
        [_dot_nt_f32(qh[h], sum_ref[:, h, :] * (1.0 / blk)) for h in range(nh)], axis=0)
    bl = lax.broadcasted_iota(I32, gate.shape, 1)
    big = jnp.int32(1 << 20)
    sel = jnp.zeros(gate.shape, F32)
    g = gate
    for _ in range(min(MOBA_TOPK, nblk)):
        gm = jnp.max(g, axis=-1, keepdims=True)
        gi = jnp.min(jnp.where(g == gm, bl, big), axis=-1, keepdims=True)
        hit = bl == gi
        sel = jnp.where(hit, 1.0, sel)
        g = jnp.where(hit, NEG * 2.0, g)
    kpos_blk = lax.broadcasted_iota(I32, (nblk, past), 1) // blk
    expand = jnp.where(kpos_blk == lax.broadcasted_iota(I32, (nblk, past), 0), 1.0, 0.0).astype(BF16)
    keep = _dot(sel.astype(BF16), expand)

    trow = trow_ref[...]
    col = lambda b_: trow[:, b_:b_ + 1]
    t_of_row = lax.broadcasted_iota(I32, (rows, LANES), 0) & (t_new - 1)
    kpos_tail = (past - LANES) + lax.broadcasted_iota(I32, (rows, LANES), 1)
    bias_tail = _t5_bias(past + t_of_row - kpos_tail, col)
    s = s_buf[...] + col(REL_BUCKETS - 1)
    s_tail = s_buf[:, past - LANES:] + bias_tail
    s = jnp.concatenate([s[:, :past - LANES], s_tail], axis=-1)
    s = jnp.where(keep > 0.5, s, NEG)

    kn = kn_ref[...].astype(BF16)
    vn = vn_ref[...].astype(BF16)
    s_own = jnp.concatenate([_dot_nt(qh_b[h], kn[:, h * dh:(h + 1) * dh]) for h in range(nh)], axis=0) * scale
    r_t = lax.broadcasted_iota(I32, s_own.shape, 0) & (t_new - 1)
    c_t = lax.broadcasted_iota(I32, s_own.shape, 1)
    s_own = s_own + _t5_bias(jnp.maximum(r_t - c_t, 0), col)
    s_own = jnp.where(c_t <= r_t, s_own, NEG)

    m = jnp.maximum(jnp.max(s, axis=-1, keepdims=True), jnp.max(s_own, axis=-1, keepdims=True))
    p = jnp.exp(s - m)
    p_own = jnp.exp(s_own - m)
    l = jnp.sum(p, axis=-1, keepdims=True) + jnp.sum(p_own, axis=-1, keepdims=True)
    s_buf[...] = p
    p_own = p_own.astype(BF16)
    acc = [_dot(p_own[h * t_new:(h + 1) * t_new, :], vn[:, h * dh:(h + 1) * dh]) for h in range(nh)]

    for ci in range(n_chunks):
        sl = (n_chunks + ci) % 2
        if ci + 1 < n_chunks:
            for cp in copies(b, ci + 1, v_hbm, 1 - sl):
                cp.start()
        else:
            @pl.when(b + 1 < nb)
            def _():
                for cp in copies(b + 1, 0, k_hbm, 1 - sl):
                    cp.start()
        for cp in copies(b, ci, v_hbm, sl):
            cp.wait()
        for h in range(nh):
            vh = buf[sl, pl.ds(h, chunk, stride=nh), :].astype(BF16)
            ph = s_buf[h * t_new:(h + 1) * t_new, ci * chunk:(ci + 1) * chunk].astype(BF16)
            acc[h] = acc[h] + _dot(ph, vh)
    for h in range(nh):
        o_ref[:, h * dh:(h + 1) * dh] = acc[h] / l[h * t_new:(h + 1) * t_new, :]


def _moba_sample_attn(page_table, q_all, k_s, v_s, rel_table, cache_k, cache_v, layer, n_p, t_new):
    n_seq, n_pages = page_table.shape
    n_layers, n_phys, page, nh, dh = cache_k.shape
    d = nh * dh
    past = n_pages * page
    cpages = min(8, n_pages // 2)
    assert n_pages % cpages == 0 and (n_pages // cpages) % 2 == 0
    rows = nh * t_new
    q_off = n_p // t_new
    trow = jnp.repeat(rel_table.T, t_new, axis=0)
    cache_k = cache_k.reshape(n_layers, n_phys, page * nh, dh)
    cache_v = cache_v.reshape(n_layers, n_phys, page * nh, dh)
    grid_spec = pltpu.PrefetchScalarGridSpec(
        num_scalar_prefetch=1,
        grid=(n_seq,),
        in_specs=[pl.BlockSpec((t_new, d), lambda b, pt: (q_off + b, 0)),
                  pl.BlockSpec((t_new, d), lambda b, pt: (b, 0)),
                  pl.BlockSpec((t_new, d), lambda b, pt: (b, 0)),
                  pl.BlockSpec((rows, REL_BUCKETS), lambda b, pt: (0, 0)),
                  pl.BlockSpec(memory_space=pl.ANY),
                  pl.BlockSpec(memory_space=pl.ANY)],
        out_specs=pl.BlockSpec((t_new, d), lambda b, pt: (b, 0)),
        scratch_shapes=[pltpu.VMEM((2, cpages * page * nh, dh), F32),
                        pltpu.VMEM((past // MOBA_BLOCK, nh, dh), F32),
                        pltpu.VMEM((rows, past), F32),
                        pltpu.SemaphoreType.DMA((2,))])
    return pl.pallas_call(
        functools.partial(_moba_sample_kernel, layer, n_pages, page, t_new, cpages),
        grid_spec=grid_spec,
        out_shape=jax.ShapeDtypeStruct((n_seq * t_new, d), F32),
        compiler_params=_cparams(("arbitrary",)),
        name="moba_sample_attn",
    )(page_table, q_all, k_s, v_s, trow, cache_k, cache_v)


def _rope_tables(pos, tile_rows=None):
    half = MLA_ROPE // 2
    inv_freq = ROPE_THETA ** (-jnp.arange(half, dtype=F32) / half)
    ang = pos.astype(F32)[:, None] * inv_freq
    z = jnp.zeros((pos.shape[0], LANES - MLA_ROPE), F32)
    cos = jnp.concatenate([jnp.cos(ang), jnp.cos(ang), z], axis=-1)
    sin = jnp.concatenate([jnp.sin(ang), jnp.sin(ang), z], axis=-1)
    return cos, sin


def kernel(x_prompt, x_sample, cache_mla_ckv, cache_mla_kpe, cache_moba_k, cache_moba_v, page_table, c_prompt, c_sample, w_ada, b_ada, g_norm_mix, g_norm_ffn, g_final, w_dq, g_q, w_uq, w_dkv, g_kv, w_uk, w_uv, w_o_mla, w_qkv_moba, w_o_moba, rel_bias_table, w_router_group, w_router_expert, w_exp_gate, w_exp_up, w_exp_down):
    n_batch, seq_len, d = x_prompt.shape
    n_seq, t_new, _ = x_sample.shape
    depth = w_ada.shape[0]
    page = cache_mla_ckv.shape[2]
    past = page_table.shape[1] * page
    n_p, n_s = n_batch * seq_len, n_seq * t_new
    tile = min(TOK_TILE, n_s)
    assert n_s % tile == 0 and seq_len % tile == 0 and tile % t_new == 0

    xp = x_prompt.reshape(n_p, d)
    xs = x_sample.reshape(n_s, d)
    m = _ada(jnp.concatenate([c_sample, c_prompt], axis=0), w_ada, b_ada)
    mp4 = m[:, n_seq:, :].reshape(depth, n_batch, 1, 6 * d)
    ms3 = jnp.repeat(m[:, :n_seq, :], t_new, axis=1)

    cos_p, sin_p = _rope_tables(jnp.arange(seq_len))
    cos_s, sin_s = _rope_tables(past + (jnp.arange(tile) % t_new))
    cos_all = jnp.concatenate([cos_p, cos_s], axis=0)
    sin_all = jnp.concatenate([sin_p, sin_s], axis=0)

    outs = {}
    x = (xp, xs)
    for i in range(depth):
        if i % 2 == 0:
            la = i // 2
            if not isinstance(x, tuple):
                x = (x[:n_p], x[n_p:])
            ckv_p, ckv_s, kpe_p, kpe_s, q_all, k_all, q_s, k_s = _mla_proj(
                x[0], x[1], mp4, ms3, i, g_norm_mix[i], cos_all, sin_all,
                w_dq[la], g_q[la], w_uq[la], w_dkv[la], g_kv[la], w_uk[la], seq_len, tile)
            o_p = _mla_prompt_attn(q_all, k_all, n_batch, seq_len, min(ATT_TILE, seq_len))
            o_s = _mla_sample_attn(page_table, q_s, k_s, cache_mla_ckv, cache_mla_kpe, la, t_new)
            outs.setdefault("ckv_p", []).append(ckv_p.reshape(n_batch, seq_len, MLA_KV_RANK))
            outs.setdefault("kpe_p", []).append(kpe_p.reshape(n_batch, seq_len, MLA_ROPE))
            outs.setdefault("ckv_s", []).append(ckv_s.reshape(n_seq, t_new, MLA_KV_RANK))
            outs.setdefault("kpe_s", []).append(kpe_s.reshape(n_seq, t_new, MLA_ROPE))
            x_new, h2, route = _attn_out(o_p, o_s, x, mp4, ms3, i, g_norm_ffn[i], w_uv[la], w_o_mla[la],
                                         w_router_group[i], w_router_expert[i], seq_len, tile)
        else:
            lb = i // 2
            xu = x if not isinstance(x, tuple) else jnp.concatenate(x, axis=0)
            q_all, k_p, k_s, v_p, v_s = _moba_qkv(xu, mp4, ms3, i, g_norm_mix[i], w_qkv_moba[lb], n_p, seq_len, tile)
            o_p = _moba_prompt_attn(q_all, k_p, v_p, rel_bias_table, n_batch, seq_len)
            o_s = _moba_sample_attn(page_table, q_all, k_s, v_s, rel_bias_table, cache_moba_k, cache_moba_v,
                                    lb, n_p, t_new)
            hd = (MOBA_HEADS, MOBA_HEAD_DIM)
            outs.setdefault("k_p", []).append(k_p.reshape(n_batch, seq_len, *hd))
            outs.setdefault("v_p", []).append(v_p.reshape(n_batch, seq_len, *hd))
            outs.setdefault("k_s", []).append(k_s.reshape(n_seq, t_new, *hd))
            outs.setdefault("v_s", []).append(v_s.reshape(n_seq, t_new, *hd))
            x_new, h2, route = _attn_out(o_p, o_s, xu, mp4, ms3, i, g_norm_ffn[i], None, w_o_moba[lb],
                                         w_router_group[i], w_router_expert[i], seq_len, tile)
        y2 = _moe(h2, route, w_exp_gate[i], w_exp_up[i], w_exp_down[i], MOE_TILE)
        last = i == depth - 1
        x = _combine(x_new, y2, mp4, ms3, i, g_final if last else None, n_p, seq_len, tile)
    y_p, y_s = x
    st = lambda key: jnp.stack(outs[key])
    return (y_p.reshape(n_batch, seq_len, d), y_s.reshape(n_seq, t_new, d),
            st("ckv_p"), st("kpe_p"), st("k_p"), st("v_p"), st("ckv_s"), st("kpe_s"), st("k_s"), st("v_s"))
```

```python
import functools
import math

import numpy as np
import jax
import jax.numpy as jnp
from jax import lax
from jax.experimental import pallas as pl
from jax.experimental.pallas import tpu as pltpu

F32 = jnp.float32
BF16 = jnp.bfloat16
I32 = jnp.int32

EPS = 1e-6
ROPE_THETA = 10000.0
MLA_HEADS = 8
MLA_NOPE = 128
MLA_ROPE = 64
MLA_KV_RANK = 256
MLA_V = 128
MLA_SCALE = (MLA_NOPE + MLA_ROPE) ** -0.5
MLA_QK = MLA_KV_RANK + 128
MOBA_HEADS = 8
MOBA_HEAD_DIM = 128
MOBA_BLOCK = 256
MOBA_TOPK = 3
REL_BUCKETS = 32
REL_MAX_DIST = 128
N_GROUPS = 4
EXPERTS_PER_GROUP = 8
N_EXPERTS = N_GROUPS * EXPERTS_PER_GROUP
LANES = 128
NEG = -1e30

TOK_TILE = 512
ATT_TILE = 256
MOE_TILE = 256
VMEM_LIMIT = 56 * 1024 * 1024


def _t5_thresholds():
    max_exact = REL_BUCKETS // 2
    d = np.arange(1, 4 * REL_MAX_DIST).astype(np.float32)
    t = (np.log(d / np.float32(max_exact)) / np.float32(math.log(REL_MAX_DIST / max_exact))
         * np.float32(REL_BUCKETS - max_exact)).astype(np.float32)
    b = np.where(d < max_exact, d.astype(np.int32), np.minimum(max_exact + t.astype(np.int32), REL_BUCKETS - 1))
    return [int(d[np.argmax(b >= k)]) for k in range(REL_BUCKETS)]


T5_THR = _t5_thresholds()


def _cparams(sem):
    return pltpu.CompilerParams(dimension_semantics=sem, vmem_limit_bytes=VMEM_LIMIT)


def _rms(x, g):
    return x * lax.rsqrt(jnp.mean(x * x, axis=-1, keepdims=True) + EPS) * g


def _sigmoid(x):
    return 1.0 / (1.0 + jnp.exp(-x))


def _dot(a, b):
    return jnp.dot(a, b, preferred_element_type=F32)


def _dot_nt(a, b):
    return lax.dot_general(a, b, (((1,), (1,)), ((), ())), preferred_element_type=F32)


def _dot_f32(a, b):
    return jnp.dot(a, b, preferred_element_type=F32, precision=lax.Precision.HIGHEST)


def _dot_nt_f32(a, b):
    return lax.dot_general(a, b, (((1,), (1,)), ((), ())), preferred_element_type=F32,
                           precision=lax.Precision.HIGHEST)


def _split_specs(n_pt, cols, tile):
    return [pl.BlockSpec((tile, cols), lambda i: (jnp.minimum(i, n_pt - 1), 0)),
            pl.BlockSpec((tile, cols), lambda i: (jnp.maximum(i - n_pt, 0), 0))]


def _mod_specs(layer, k, n_pt, tiles_per_seq, n_batch, d, tile):
    return [pl.BlockSpec((None, None, 1, d),
                         lambda i: (layer, jnp.minimum(i // tiles_per_seq, n_batch - 1), 0, k)),
            pl.BlockSpec((None, tile, d), lambda i: (layer, jnp.maximum(i - n_pt, 0), k))]


def _full(shape):
    nd = len(shape)
    return pl.BlockSpec(shape, lambda *_: (0,) * nd)


def _ada_kernel(c_ref, w_ref, b_ref, o_ref):
    c = c_ref[...]
    s = c * _sigmoid(c)
    o_ref[...] = _dot(s.astype(BF16), w_ref[...].astype(BF16)) + b_ref[...]


def _ada(c_all, w_ada, b_ada):
    depth, d, d6 = w_ada.shape
    n = c_all.shape[0]
    cols = 1536
    return pl.pallas_call(
        _ada_kernel,
        grid=(depth, d6 // cols),
        in_specs=[pl.BlockSpec((n, d), lambda l, j: (0, 0)),
                  pl.BlockSpec((None, d, cols), lambda l, j: (l, 0, j)),
                  pl.BlockSpec((None, 1, cols), lambda l, j: (l, 0, j))],
        out_specs=pl.BlockSpec((None, n, cols), lambda l, j: (l, 0, j)),
        out_shape=jax.ShapeDtypeStruct((depth, n, d6), F32),
        compiler_params=_cparams(("arbitrary", "arbitrary")),
        name="ada_params",
    )(c_all, w_ada, b_ada.reshape(depth, 1, d6))


def _mla_proj_kernel(n_pt, xp_ref, xs_ref, shp_ref, shs_ref, scp_ref, scs_ref, gmix_ref, cos_ref, sin_ref,
                     wdq_ref, gq_ref, wuqn_ref, wuqp_ref, wuqr_ref, wuk_ref, wdkv_ref, wdkvr_ref, gkv_ref,
                     ckvp_ref, ckvs_ref, kpep_ref, kpes_ref, q_ref, k_ref, qs_ref, ks_ref):
    i = pl.program_id(0)
    is_s = i >= n_pt
    x = jnp.where(is_s, xs_ref[...], xp_ref[...])
    shift = jnp.where(is_s, shs_ref[...], shp_ref[...])
    scale = jnp.where(is_s, scs_ref[...], scp_ref[...])
    h = (_rms(x, gmix_ref[...]) * (1.0 + scale) + shift).astype(BF16)
    cos = cos_ref[...]
    sin = sin_ref[...]

    cq = _dot(h, wdq_ref[...])
    cqn = _rms(cq, gq_ref[...]).astype(BF16)
    qn = _dot(cqn, wuqn_ref[...]).astype(BF16)
    qpe = _dot(cqn, wuqp_ref[...])
    qrot = _dot(cqn, wuqr_ref[...])
    kv = _dot(h, wdkv_ref[...])
    kvrot = _dot(h, wdkvr_ref[...])
    ckv = _rms(kv[:, :MLA_KV_RANK], gkv_ref[...])
    kpe = kv[:, MLA_KV_RANK:] * cos + kvrot * sin

    pieces = []
    for hh in range(MLA_HEADS):
        sl = slice(hh * LANES, (hh + 1) * LANES)
        pieces.append(_dot(qn[:, sl], wuk_ref[hh]))
        pieces.append(qpe[:, sl] * cos + qrot[:, sl] * sin)
    q = jnp.concatenate(pieces, axis=-1)
    k = jnp.concatenate([ckv, kpe], axis=-1)
    q_ref[...] = q.astype(BF16)
    k_ref[...] = k.astype(BF16)

    @pl.when(jnp.logical_not(is_s))
    def _():
        ckvp_ref[...] = ckv
        kpep_ref[...] = kpe[:, :MLA_ROPE]

    @pl.when(is_s)
    def _():
        ckvs_ref[...] = ckv
        kpes_ref[...] = kpe[:, :MLA_ROPE]
        qs_ref[...] = q
        ks_ref[...] = k


def _rot_cols(w):
    half = MLA_ROPE // 2
    return jnp.concatenate([-w[..., half:], w[..., :half]], axis=-1)


def _mla_proj(xp, xs, mp4, ms3, layer, g_mix, cos_all, sin_all, w_dq, g_q, w_uq, w_dkv, g_kv, w_uk,
              seq_len, tile):
    n_p, d = xp.shape
    n_s = xs.shape[0]
    n_pt, n_st = n_p // tile, n_s // tile
    tps = seq_len // tile
    n_batch = n_p // seq_len
    qr = w_dq.shape[1]
    hq = MLA_HEADS * LANES
    wq = w_uq.reshape(qr, MLA_HEADS, MLA_NOPE + MLA_ROPE)
    w_n = wq[:, :, :MLA_NOPE].reshape(qr, hq).astype(BF16)
    zpad = jnp.zeros((qr, MLA_HEADS, LANES - MLA_ROPE), F32)
    w_p = jnp.concatenate([wq[:, :, MLA_NOPE:], zpad], axis=-1).reshape(qr, hq).astype(BF16)
    w_r = jnp.concatenate([_rot_cols(wq[:, :, MLA_NOPE:]), zpad], axis=-1).reshape(qr, hq).astype(BF16)
    w_ukt = jnp.transpose(w_uk, (1, 2, 0)).astype(BF16)
    zk = jnp.zeros((d, LANES - MLA_ROPE), F32)
    w_kv = jnp.concatenate([w_dkv, zk], axis=-1).astype(BF16)
    w_kvr = jnp.concatenate([_rot_cols(w_dkv[:, MLA_KV_RANK:]), zk], axis=-1).astype(BF16)
    n_tab = cos_all.shape[0] // tile - 1

    tab_spec = pl.BlockSpec((tile, LANES), lambda i: (jnp.where(i < n_pt, i % tps, n_tab), 0))
    in_specs = (_split_specs(n_pt, d, tile)
                + _mod_specs(layer, 0, n_pt, tps, n_batch, d, tile)
                + _mod_specs(layer, 1, n_pt, tps, n_batch, d, tile)
                + [_full((1, d)), tab_spec, tab_spec,
                   _full((d, qr)), _full((1, qr)), _full((qr, hq)), _full((qr, hq)), _full((qr, hq)),
                   _full((MLA_HEADS, MLA_NOPE, MLA_KV_RANK)), _full((d, MLA_QK)), _full((d, LANES)),
                   _full((1, MLA_KV_RANK))])
    n_all = n_p + n_s
    out_shape = (jax.ShapeDtypeStruct((n_p, MLA_KV_RANK), F32), jax.ShapeDtypeStruct((n_s, MLA_KV_RANK), F32),
                 jax.ShapeDtypeStruct((n_p, MLA_ROPE), F32), jax.ShapeDtypeStruct((n_s, MLA_ROPE), F32),
                 jax.ShapeDtypeStruct((n_all, MLA_HEADS * MLA_QK), BF16),
                 jax.ShapeDtypeStruct((n_all, MLA_QK), BF16),
                 jax.ShapeDtypeStruct((n_s, MLA_HEADS * MLA_QK), F32),
                 jax.ShapeDtypeStruct((n_s, MLA_QK), F32))
    out_specs = (_split_specs(n_pt, MLA_KV_RANK, tile) + _split_specs(n_pt, MLA_ROPE, tile)
                 + [pl.BlockSpec((tile, MLA_HEADS * MLA_QK), lambda i: (i, 0)),
                    pl.BlockSpec((tile, MLA_QK), lambda i: (i, 0)),
                    _split_specs(n_pt, MLA_HEADS * MLA_QK, tile)[1],
                    _split_specs(n_pt, MLA_QK, tile)[1]])
    return pl.pallas_call(
        functools.partial(_mla_proj_kernel, n_pt),
        grid=(n_pt + n_st,),
        in_specs=in_specs, out_specs=out_specs, out_shape=out_shape,
        compiler_params=_cparams(("arbitrary",)),
        name="mla_proj",
    )(xp, xs, mp4, ms3, mp4, ms3, g_mix.reshape(1, d), cos_all, sin_all,
      w_dq.astype(BF16), g_q.reshape(1, qr), w_n, w_p, w_r, w_ukt, w_kv, w_kvr, g_kv.reshape(1, MLA_KV_RANK))


def _mla_prompt_kernel(tq, q_ref, k_ref, o_ref, acc_ref):
    i = pl.program_id(1)
    m_rows = MLA_HEADS * tq
    q = q_ref[...]
    qs = jnp.concatenate([q[:, h * MLA_QK:(h + 1) * MLA_QK] for h in range(MLA_HEADS)], axis=0)

    def step(kblk, m, l, masked):
        s = _dot_nt(qs, kblk) * MLA_SCALE
        if masked:
            r = lax.broadcasted_iota(I32, s.shape, 0) & (tq - 1)
            c = lax.broadcasted_iota(I32, s.shape, 1)
            s = jnp.where(c <= r, s, NEG)
        m_new = jnp.maximum(m, jnp.max(s, axis=-1, keepdims=True))
        alpha = jnp.exp(m - m_new)
        p = jnp.exp(s - m_new)
        l_new = alpha * l + jnp.sum(p, axis=-1, keepdims=True)
        pv = _dot(p.astype(BF16), kblk[:, :MLA_KV_RANK])
        return m_new, l_new, alpha, pv

    kd = k_ref[pl.ds(pl.multiple_of(i * tq, tq), tq), :]
    m0 = jnp.full((m_rows, 1), NEG, F32)
    l0 = jnp.zeros((m_rows, 1), F32)
    m, l, _, pv = step(kd, m0, l0, True)
    acc_ref[...] = pv

    def body(j, carry):
        m, l = carry
        kb = k_ref[pl.ds(pl.multiple_of(j * tq, tq), tq), :]
        m, l, alpha, pv = step(kb, m, l, False)
        acc_ref[...] = acc_ref[...] * alpha + pv
        return m, l

    m, l = lax.fori_loop(0, i, body, (m, l))
    o = acc_ref[...] / l
    for h in range(MLA_HEADS):
        o_ref[:, h * MLA_KV_RANK:(h + 1) * MLA_KV_RANK] = o[h * tq:(h + 1) * tq, :].astype(o_ref.dtype)


def _mla_prompt_attn(q_all, k_all, n_batch, seq_len, tq):
    nq = seq_len // tq
    return pl.pallas_call(
        functools.partial(_mla_prompt_kernel, tq),
        grid=(n_batch, nq),
        in_specs=[pl.BlockSpec((tq, MLA_HEADS * MLA_QK), lambda b, i: (b * nq + i, 0)),
                  pl.BlockSpec((seq_len, MLA_QK), lambda b, i: (b, 0))],
        out_specs=pl.BlockSpec((tq, MLA_HEADS * MLA_KV_RANK), lambda b, i: (b * nq + i, 0)),
        out_shape=jax.ShapeDtypeStruct((n_batch * seq_len, MLA_HEADS * MLA_KV_RANK), BF16),
        scratch_shapes=[pltpu.VMEM((MLA_HEADS * tq, MLA_KV_RANK), F32)],
        compiler_params=_cparams(("arbitrary", "arbitrary")),
        name="mla_prompt_attn",
    )(q_all, k_all)


def _mla_sample_kernel(layer, n_pages, page, t_new, chunk,
                       pt_ref, q_ref, k_ref, ckv_hbm, kpe_hbm, o_ref,
                       ckv_buf, kpe_buf, kb_buf, s_buf, sem):
    b = pl.program_id(0)
    nb = pl.num_programs(0)
    slot = b % 2
    past = n_pages * page

    def copies(seq, sl):
        out = []
        for j in range(n_pages):
            pid = pt_ref[seq, j]
            out.append(pltpu.make_async_copy(ckv_hbm.at[layer, pid], ckv_buf.at[sl, pl.ds(j * page, page)], sem.at[0, sl]))
            out.append(pltpu.make_async_copy(kpe_hbm.at[layer, pid], kpe_buf.at[sl, j], sem.at[1, sl]))
        return out

    @pl.when(b == 0)
    def _():
        for cp in copies(0, 0):
            cp.start()

    @pl.when(b + 1 < nb)
    def _():
        for cp in copies(b + 1, 1 - slot):
            cp.start()

    for cp in copies(b, slot):
        cp.wait()

    q = q_ref[...]
    qs = jnp.concatenate([q[:, h * MLA_QK:(h + 1) * MLA_QK] for h in range(MLA_HEADS)], axis=0).astype(BF16)
    ql = qs[:, :MLA_KV_RANK]
    qp = qs[:, MLA_KV_RANK:MLA_KV_RANK + MLA_ROPE]
    ppc = chunk // page
    for c in range(past // chunk):
        kc = ckv_buf[slot, c * chunk:(c + 1) * chunk, :].astype(BF16)
        kb_buf[c * chunk:(c + 1) * chunk, :] = kc
        s_pe = jnp.concatenate([_dot(qp, kpe_buf[slot, c * ppc + j].astype(BF16)) for j in range(ppc)], axis=-1)
        s_buf[:, c * chunk:(c + 1) * chunk] = (_dot_nt(ql, kc) + s_pe) * MLA_SCALE

    kn = k_ref[...].astype(BF16)
    s_new = _dot_nt(qs, kn) * MLA_SCALE
    r = lax.broadcasted_iota(I32, s_new.shape, 0) & (t_new - 1)
    c_ = lax.broadcasted_iota(I32, s_new.shape, 1)
    s_new = jnp.where(c_ <= r, s_new, NEG)
    s = s_buf[...]
    m = jnp.maximum(jnp.max(s, axis=-1, keepdims=True), jnp.max(s_new, axis=-1, keepdims=True))
    p_new = jnp.exp(s_new - m)
    l = jnp.sum(p_new, axis=-1, keepdims=True)
    acc = _dot(p_new.astype(BF16), kn[:, :MLA_KV_RANK])
    for c in range(past // chunk):
        p = jnp.exp(s_buf[:, c * chunk:(c + 1) * chunk] - m)
        l = l + jnp.sum(p, axis=-1, keepdims=True)
        acc = acc + _dot(p.astype(BF16), kb_buf[c * chunk:(c + 1) * chunk, :])
    o = acc / l
    for h in range(MLA_HEADS):
        o_ref[:, h * MLA_KV_RANK:(h + 1) * MLA_KV_RANK] = o[h * t_new:(h + 1) * t_new, :]


def _mla_sample_attn(page_table, q_s, k_s, cache_ckv, cache_kpe, layer, t_new):
    n_seq, n_pages = page_table.shape
    page = cache_ckv.shape[2]
    past = n_pages * page
    chunk = min(1024, past)
    rows = MLA_HEADS * t_new
    cache_kpe = jnp.swapaxes(cache_kpe, 2, 3)
    grid_spec = pltpu.PrefetchScalarGridSpec(
        num_scalar_prefetch=1,
        grid=(n_seq,),
        in_specs=[pl.BlockSpec((t_new, MLA_HEADS * MLA_QK), lambda b, pt: (b, 0)),
                  pl.BlockSpec((t_new, MLA_QK), lambda b, pt: (b, 0)),
                  pl.BlockSpec(memory_space=pl.ANY),
                  pl.BlockSpec(memory_space=pl.ANY)],
        out_specs=pl.BlockSpec((t_new, MLA_HEADS * MLA_KV_RANK), lambda b, pt: (b, 0)),
        scratch_shapes=[pltpu.VMEM((2, past, MLA_KV_RANK), F32),
                        pltpu.VMEM((2, n_pages, MLA_ROPE, page), F32),
                        pltpu.VMEM((past, MLA_KV_RANK), BF16),
                        pltpu.VMEM((rows, past), F32),
                        pltpu.SemaphoreType.DMA((2, 2))])
    return pl.pallas_call(
        functools.partial(_mla_sample_kernel, layer, n_pages, page, t_new, chunk),
        grid_spec=grid_spec,
        out_shape=jax.ShapeDtypeStruct((n_seq * t_new, MLA_HEADS * MLA_KV_RANK), F32),
        compiler_params=_cparams(("arbitrary",)),
        name="mla_sample_attn",
    )(page_table, q_s, k_s, cache_ckv, cache_kpe)


def _route(z):
    lane = lax.broadcasted_iota(I32, z.shape, 1)
    big = jnp.int32(1 << 20)
    gmask = lane < N_GROUPS
    zg = jnp.where(gmask, z, NEG)
    gmax = jnp.max(zg, axis=-1, keepdims=True)
    gidx = jnp.min(jnp.where(jnp.logical_and(gmask, zg == gmax), lane, big), axis=-1, keepdims=True)
    gsum = jnp.sum(jnp.where(gmask, jnp.exp(zg - gmax), 0.0), axis=-1, keepdims=True)
    g_top = 1.0 / gsum
    lo = N_GROUPS + gidx * EXPERTS_PER_GROUP
    emask = jnp.logical_and(lane >= lo, lane < lo + EXPERTS_PER_GROUP)
    z1 = jnp.where(emask, z, NEG)
    e1 = jnp.max(z1, axis=-1, keepdims=True)
    i1 = jnp.min(jnp.where(jnp.logical_and(emask, z1 == e1), lane, big), axis=-1, keepdims=True)
    z2 = jnp.where(lane == i1, NEG, z1)
    e2 = jnp.max(z2, axis=-1, keepdims=True)
    i2 = jnp.min(jnp.where(jnp.logical_and(emask, jnp.logical_and(z2 == e2, lane != i1)), lane, big),
                 axis=-1, keepdims=True)
    t = jnp.exp(e2 - e1)
    w1 = g_top / (1.0 + t)
    w2 = g_top * t / (1.0 + t)
    out = jnp.where(lane == 0, (i1 - N_GROUPS).astype(F32),
                    jnp.where(lane == 1, (i2 - N_GROUPS).astype(F32),
                              jnp.where(lane == 2, w1, jnp.where(lane == 3, w2, 0.0))))
    return out


def _attn_out_kernel(n_pt, split_x, use_uv, *refs):
    refs = list(refs)
    op_ref, os_ref = refs[:2]
    refs = refs[2:]
    if split_x:
        xp_ref, xs_ref = refs[:2]
        refs = refs[2:]
    else:
        x_ref = refs[0]
        refs = refs[1:]
    (gp_ref, gs_ref, shp_ref, shs_ref, scp_ref, scs_ref, gffn_ref) = refs[:7]
    refs = refs[7:]
    if use_uv:
        wuv_ref = refs[0]
        refs = refs[1:]
    wo_ref, wrh_ref, wrl_ref, xo_ref, h2_ref, rt_ref = refs

    i = pl.program_id(0)
    is_s = i >= n_pt
    o = jnp.where(is_s, os_ref[...].astype(BF16), op_ref[...])
    if split_x:
        x = jnp.where(is_s, xs_ref[...], xp_ref[...])
    else:
        x = x_ref[...]
    gate = jnp.where(is_s, gs_ref[...], gp_ref[...])
    shift = jnp.where(is_s, shs_ref[...], shp_ref[...])
    scale = jnp.where(is_s, scs_ref[...], scp_ref[...])
    if use_uv:
        o = jnp.concatenate(
            [_dot(o[:, h * MLA_KV_RANK:(h + 1) * MLA_KV_RANK], wuv_ref[h]) for h in range(MLA_HEADS)],
            axis=-1).astype(BF16)
    a = _dot(o, wo_ref[...])
    xn = x + gate * a
    xo_ref[...] = xn
    h2 = _rms(xn, gffn_ref[...]) * (1.0 + scale) + shift
    h2_ref[...] = h2
    h2h = h2.astype(BF16)
    h2l = (h2 - h2h.astype(F32)).astype(BF16)
    z = _dot(h2h, wrh_ref[...]) + (_dot(h2h, wrl_ref[...]) + _dot(h2l, wrh_ref[...]))
    rt_ref[...] = _route(z)


def _attn_out(o_p, o_s, x, mp4, ms3, layer, g_ffn, w_uv, w_o, w_rg, w_re, seq_len, tile):
    split_x = isinstance(x, tuple)
    n_p = o_p.shape[0]
    n_s = o_s.shape[0]
    din = o_p.shape[1]
    d = w_o.shape[1]
    n_pt, n_st = n_p // tile, n_s // tile
    tps = seq_len // tile
    n_batch = n_p // seq_len
    use_uv = w_uv is not None
    wr = jnp.concatenate([w_rg, w_re, jnp.zeros((d, LANES - N_GROUPS - N_EXPERTS), F32)], axis=-1)
    in_specs = _split_specs(n_pt, din, tile)
    args = [o_p, o_s]
    if split_x:
        in_specs += _split_specs(n_pt, d, tile)
        args += list(x)
    else:
        in_specs += [pl.BlockSpec((tile, d), lambda i: (i, 0))]
        args += [x]
    for k in (2, 3, 4):
        in_specs += _mod_specs(layer, k, n_pt, tps, n_batch, d, tile)
        args += [mp4, ms3]
    in_specs += [_full((1, d))]
    args += [g_ffn.reshape(1, d)]
    if use_uv:
        in_specs += [_full((MLA_HEADS, MLA_KV_RANK, MLA_V))]
        args += [jnp.transpose(w_uv, (1, 0, 2)).astype(BF16)]
    wr_hi = wr.astype(BF16)
    wr_lo = (wr - wr_hi.astype(F32)).astype(BF16)
    in_specs += [_full(w_o.shape), _full((d, LANES)), _full((d, LANES))]
    args += [w_o.astype(BF16), wr_hi, wr_lo]
    n_all = n_p + n_s
    tok = lambda cols: pl.BlockSpec((tile, cols), lambda i: (i, 0))
    return pl.pallas_call(
        functools.partial(_attn_out_kernel, n_pt, split_x, use_uv),
        grid=(n_pt + n_st,),
        in_specs=in_specs,
        out_specs=[tok(d), tok(d), tok(LANES)],
        out_shape=(jax.ShapeDtypeStruct((n_all, d), F32), jax.ShapeDtypeStruct((n_all, d), F32),
                   jax.ShapeDtypeStruct((n_all, LANES), F32)),
        compiler_params=_cparams(("arbitrary",)),
        name="attn_out_router",
    )(*args)


def _moe_kernel(tm, te_ref, tr_ref, src_ref, srcn_ref, dst_ref, w_ref, h_hbm, wg_ref, wu_ref, wd_ref,
                y_hbm, xbuf, ybuf, gsem, ssem):
    t = pl.program_id(0)
    nt = pl.num_programs(0)
    slot = t % 2
    rows = tr_ref[t]
    rows_next = tr_ref[jnp.minimum(t + 1, nt - 1)]
    rows_prev = tr_ref[jnp.maximum(t - 1, 0)]

    def gather_copy(idx_ref, r, sl):
        return pltpu.make_async_copy(h_hbm.at[pl.ds(idx_ref[0, 0, r], 1)], xbuf.at[sl, pl.ds(r, 1)], gsem.at[sl])

    def scatter_copy(r):
        return pltpu.make_async_copy(ybuf.at[pl.ds(r, 1)], y_hbm.at[pl.ds(dst_ref[0, 0, r], 1)], ssem.at[0])

    @pl.when(jnp.logical_and(t == 0, rows > 0))
    def _():
        for r in range(tm):
            gather_copy(src_ref, r, 0).start()

    @pl.when(jnp.logical_and(t + 1 < nt, rows_next > 0))
    def _():
        for r in range(tm):
            gather_copy(srcn_ref, r, 1 - slot).start()

    @pl.when(jnp.logical_and(t > 0, rows_prev > 0))
    def _():
        for r in range(tm):
            scatter_copy(r).wait()

    @pl.when(rows > 0)
    def _():
        for r in range(tm):
            gather_copy(src_ref, r, slot).wait()
        xb = xbuf[slot].astype(BF16)
        a = _dot(xb, wg_ref[...].astype(BF16))
        u = _dot(xb, wu_ref[...].astype(BF16))
        act = (a * _sigmoid(a)) * u * w_ref[...]
        ybuf[...] = _dot(act.astype(BF16), wd_ref[...].astype(BF16))
        for r in range(tm):
            scatter_copy(r).start()

    @pl.when(t == nt - 1)
    def _():
        @pl.when(rows > 0)
        def _():
            for r in range(tm):
                scatter_copy(r).wait()
        ybuf[...] = jnp.zeros(ybuf.shape, ybuf.dtype)
        fill = pltpu.make_async_copy(ybuf, y_hbm.at[pl.ds(y_hbm.shape[0] - tm, tm)], ssem.at[0])
        fill.start()
        fill.wait()


def _moe(h2, route, w_gate, w_up, w_down, layer, tm):
    n, d = h2.shape
    _, n_exp, _, f = w_gate.shape
    eid = route[:, :2].astype(I32)
    ew = route[:, 2:4]
    n2 = 2 * n
    flat = eid.reshape(n2)
    order = jnp.argsort(flat, stable=True).astype(I32)
    counts = jnp.sum((flat[:, None] == jnp.arange(n_exp, dtype=I32)[None, :]).astype(I32), axis=0)
    padded = ((counts + tm - 1) // tm) * tm
    pend = jnp.cumsum(padded)
    pstart = pend - padded
    ustart = jnp.cumsum(counts) - counts
    n_tiles = n2 // tm + n_exp
    tstart = jnp.arange(n_tiles, dtype=I32) * tm
    te = jnp.minimum(jnp.sum((pend[None, :] <= tstart[:, None]).astype(I32), axis=1), n_exp - 1)
    r0 = tstart - pstart[te]
    tr = jnp.clip(counts[te] - r0, 0, tm).astype(I32)
    lane = jnp.arange(tm, dtype=I32)[None, :]
    pos = (ustart[te] + r0)[:, None] + lane
    fidx = order[jnp.clip(pos, 0, n2 - 1)]
    tok = fidx >> 1
    src = tok.reshape(n_tiles, 1, tm)
    dst = jnp.where(lane < tr[:, None], (fidx & 1) * n + tok, n2 + lane).reshape(n_tiles, 1, tm)
    wslot = ew.reshape(n2)[fidx].reshape(n_tiles, tm, 1)

    smem_blk = lambda f_: pl.BlockSpec((1, 1, tm), f_, memory_space=pltpu.SMEM)
    wspec = lambda a_, b_: pl.BlockSpec((None, None, a_, b_), lambda t, te_, tr_: (layer, te_[t], 0, 0))
    grid_spec = pltpu.PrefetchScalarGridSpec(
        num_scalar_prefetch=2,
        grid=(n_tiles,),
        in_specs=[smem_blk(lambda t, te_, tr_: (t, 0, 0)),
                  smem_blk(lambda t, te_, tr_: (jnp.minimum(t + 1, n_tiles - 1), 0, 0)),
                  smem_blk(lambda t, te_, tr_: (t, 0, 0)),
                  pl.BlockSpec((None, tm, 1), lambda t, te_, tr_: (t, 0, 0)),
                  pl.BlockSpec(memory_space=pl.ANY),
                  wspec(d, f), wspec(d, f), wspec(f, d)],
        out_specs=pl.BlockSpec(memory_space=pl.ANY),
        scratch_shapes=[pltpu.VMEM((2, tm, d), F32), pltpu.VMEM((tm, d), F32),
                        pltpu.SemaphoreType.DMA((2,)), pltpu.SemaphoreType.DMA((1,))])
    return pl.pallas_call(
        functools.partial(_moe_kernel, tm),
        grid_spec=grid_spec,
        out_shape=jax.ShapeDtypeStruct((n2 + tm, d), F32),
        compiler_params=_cparams(("arbitrary",)),
        name="moe_experts",
    )(te, tr, src, src, dst, wslot, h2, w_gate, w_up, w_down)


def _combine_kernel(n_pt, final, d, *refs):
    if final:
        x_ref, y0_ref, y1_ref, gp_ref, gs_ref, gf_ref, op_ref, os_ref = refs
    else:
        x_ref, y0_ref, y1_ref, gp_ref, gs_ref, o_ref = refs
    i = pl.program_id(0)
    is_s = i >= n_pt
    gate = jnp.where(is_s, gs_ref[...], gp_ref[...])
    xn = x_ref[...] + gate * (y0_ref[...] + y1_ref[...])
    if final:
        out = _rms(xn, gf_ref[...])

        @pl.when(jnp.logical_not(is_s))
        def _():
            op_ref[...] = out

        @pl.when(is_s)
        def _():
            os_ref[...] = out
    else:
        o_ref[...] = xn


def _combine(x, y2, mp4, ms3, layer, g_final, n_p, seq_len, tile):
    n_all, d = x.shape
    n_s = n_all - n_p
    n_pt, n_st = n_p // tile, n_s // tile
    tps = seq_len // tile
    n_batch = n_p // seq_len
    final = g_final is not None
    tok = lambda cols: pl.BlockSpec((tile, cols), lambda i: (i, 0))
    n_tt = n_all // tile
    in_specs = ([tok(d), tok(d), pl.BlockSpec((tile, d), lambda i: (n_tt + i, 0))]
                + _mod_specs(layer, 5, n_pt, tps, n_batch, d, tile))
    args = [x, y2, y2, mp4, ms3]
    if final:
        in_specs += [_full((1, d))]
        args += [g_final.reshape(1, d)]
        out_specs = _split_specs(n_pt, d, tile)
        out_shape = (jax.ShapeDtypeStruct((n_p, d), F32), jax.ShapeDtypeStruct((n_s, d), F32))
    else:
        out_specs = tok(d)
        out_shape = jax.ShapeDtypeStruct((n_all, d), F32)
    return pl.pallas_call(
        functools.partial(_combine_kernel, n_pt, final, d),
        grid=(n_pt + n_st,),
        in_specs=in_specs, out_specs=out_specs, out_shape=out_shape,
        compiler_params=_cparams(("arbitrary",)),
        name="moe_combine",
    )(*args)


def _moba_qkv_kernel(n_pt, d, x_ref, shp_ref, shs_ref, scp_ref, scs_ref, gmix_ref, w_ref,
                     q_ref, kp_ref, ks_ref, vp_ref, vs_ref, kb_ref, vb_ref, ksum_ref):
    i = pl.program_id(0)
    is_s = i >= n_pt
    shift = jnp.where(is_s, shs_ref[...], shp_ref[...])
    scale = jnp.where(is_s, scs_ref[...], scp_ref[...])
    h = (_rms(x_ref[...], gmix_ref[...]) * (1.0 + scale) + shift).astype(BF16)
    qkv = _dot(h, w_ref[...])
    q_ref[...] = qkv[:, :d]

    @pl.when(jnp.logical_not(is_s))
    def _():
        k = qkv[:, d:2 * d]
        v = qkv[:, 2 * d:]
        kp_ref[...] = k
        vp_ref[...] = v
        vb_ref[...] = v.astype(BF16)
        nb = k.shape[0] // MOBA_BLOCK
        for n in range(nb):
            kb_ref[n] = jnp.transpose(k[n * MOBA_BLOCK:(n + 1) * MOBA_BLOCK, :]).astype(BF16)
        means = [jnp.sum(k[n * MOBA_BLOCK:(n + 1) * MOBA_BLOCK, :], axis=0, keepdims=True) * (1.0 / MOBA_BLOCK)
                 for n in range(nb)]
        ksum_ref[...] = jnp.concatenate(means + [jnp.zeros((8 - nb, d), F32)], axis=0)

    @pl.when(is_s)
    def _():
        ks_ref[...] = qkv[:, d:2 * d]
        vs_ref[...] = qkv[:, 2 * d:]


def _moba_qkv(x, mp4, ms3, layer, g_mix, w_qkv, n_p, seq_len, tile):
    n_all, d = x.shape
    n_s = n_all - n_p
    n_pt, n_st = n_p // tile, n_s // tile
    tps = seq_len // tile
    n_batch = n_p // seq_len
    tok = lambda cols: pl.BlockSpec((tile, cols), lambda i: (i, 0))
    in_specs = ([tok(d)] + _mod_specs(layer, 0, n_pt, tps, n_batch, d, tile)
                + _mod_specs(layer, 1, n_pt, tps, n_batch, d, tile) + [_full((1, d)), _full((d, 3 * d))])
    sp = _split_specs(n_pt, d, tile)
    assert tile % MOBA_BLOCK == 0 and tile // MOBA_BLOCK <= 8
    nb = tile // MOBA_BLOCK
    last = lambda i: jnp.minimum(i, n_pt - 1)
    outs = pl.pallas_call(
        functools.partial(_moba_qkv_kernel, n_pt, d),
        grid=(n_pt + n_st,),
        in_specs=in_specs,
        out_specs=[tok(d), sp[0], sp[1], sp[0], sp[1],
                   pl.BlockSpec((None, nb, d, MOBA_BLOCK), lambda i: (last(i) // tps, last(i) % tps, 0, 0)),
                   sp[0],
                   pl.BlockSpec((8, d), lambda i: (last(i), 0))],
        out_shape=(jax.ShapeDtypeStruct((n_all, d), F32),
                   jax.ShapeDtypeStruct((n_p, d), F32), jax.ShapeDtypeStruct((n_s, d), F32),
                   jax.ShapeDtypeStruct((n_p, d), F32), jax.ShapeDtypeStruct((n_s, d), F32),
                   jax.ShapeDtypeStruct((n_batch, seq_len // MOBA_BLOCK, d, MOBA_BLOCK), BF16),
                   jax.ShapeDtypeStruct((n_p, d), BF16),
                   jax.ShapeDtypeStruct((n_pt * 8, d), F32)),
        compiler_params=_cparams(("arbitrary",)),
        name="moba_qkv",
    )(x, mp4, ms3, mp4, ms3, g_mix.reshape(1, d), w_qkv.astype(BF16))
    q_all, k_p, k_s, v_p, v_s, k_b, v_b, ksum = outs
    kmean = ksum.reshape(n_pt, 8, d)[:, :nb, :].reshape(n_pt * nb, d)
    return q_all, k_p, k_s, v_p, v_s, k_b, v_b, kmean


def _t5_bias(dist, table_at):
    val = jnp.full(dist.shape, 0.0, F32) + table_at(0)
    for b in range(1, REL_BUCKETS):
        val = jnp.where(dist >= T5_THR[b], table_at(b), val)
    return val


MOBA_DROP = -32768.0


def _moba_prompt_kernel(q_ref, kt_ref, v_ref, mean_ref, tab_ref, o_ref,
                        bown_ref, bprev_ref, qa_ref, acc_ref, m_ref, l_ref):
    b = pl.program_id(0)
    i = pl.program_id(1)
    nh, dh, blk = MOBA_HEADS, MOBA_HEAD_DIM, MOBA_BLOCK
    nblk = kt_ref.shape[0]
    scale = dh ** -0.5
    ones = jnp.ones((blk, LANES), BF16)
    rep = lambda x: jnp.concatenate([x] * (blk // LANES), axis=-1)
    rowmax = lambda s: jnp.broadcast_to(jnp.max(s, axis=-1, keepdims=True), (blk, LANES))
    r = lax.broadcasted_iota(I32, (blk, blk), 0)
    c = lax.broadcasted_iota(I32, (blk, blk), 1)

    @pl.when(jnp.logical_and(b == 0, i == 0))
    def _():
        for h in range(nh):
            at = lambda bb, h=h: tab_ref[bb, h]
            bown_ref[h] = _t5_bias(jnp.maximum(r - c, 0), at)
            bprev_ref[h] = _t5_bias(blk + r - c, at)

    off = pl.multiple_of(i * blk, blk)
    bi = lax.broadcasted_iota(I32, (nblk, blk), 0)
    valid = bi < i
    for h in range(nh):
        hs = slice(h * dh, (h + 1) * dh)
        q = q_ref[:, hs]
        qb = q.astype(BF16)
        gate_t = jnp.where(valid, _dot_nt_f32(mean_ref[:, hs], q), NEG)
        rank = jnp.zeros(gate_t.shape, I32)
        for m_ in range(nblk - 1):
            gm = gate_t[m_:m_ + 1, :]
            ahead = jnp.logical_or(gm > gate_t, jnp.logical_and(gm == gate_t, m_ < bi))
            rank = rank + jnp.where(jnp.logical_and(ahead, m_ < i), 1, 0)
        drop_t = jnp.where(jnp.logical_and(valid, rank < MOBA_TOPK), 0.0, MOBA_DROP)
        drop = jnp.transpose(jnp.concatenate([drop_t, jnp.zeros((LANES - nblk, blk), F32)], axis=0))
        qa_ref[h] = jnp.concatenate([qb, drop.astype(BF16)], axis=-1)

        vo = jnp.concatenate([v_ref[pl.ds(off, blk), hs], ones], axis=-1)
        s = _dot(qb, kt_ref[i, hs, :]) * scale + bown_ref[h]
        s = jnp.where(c <= r, s, NEG)
        m = rowmax(s)
        p = jnp.exp(s - rep(m))
        pv = _dot(p.astype(BF16), vo)
        m_ref[h] = m
        l_ref[h] = pv[:, dh:]
        acc_ref[h] = pv[:, :dh]

    srow = lax.broadcasted_iota(I32, (LANES, blk), 0)

    def body(j, carry):
        offj = pl.multiple_of(j * blk, blk)
        onehot_t = jnp.where(srow == j, 1.0, 0.0).astype(BF16)
        is_prev = j == i - 1
        for h in range(nh):
            hs = slice(h * dh, (h + 1) * dh)
            kj = jnp.concatenate([kt_ref[j, hs, :], onehot_t], axis=0)
            vj = jnp.concatenate([v_ref[pl.ds(offj, blk), hs], ones], axis=-1)
            bias = jnp.where(is_prev, bprev_ref[h], tab_ref[REL_BUCKETS - 1, h])
            s = _dot(qa_ref[h], kj) * scale + bias
            m = m_ref[h]
            m_new = jnp.maximum(m, rowmax(s))
            alpha = jnp.exp(m - m_new)
            p = jnp.exp(s - rep(m_new))
            pv = _dot(p.astype(BF16), vj)
            m_ref[h] = m_new
            l_ref[h] = alpha * l_ref[h] + pv[:, dh:]
            acc_ref[h] = acc_ref[h] * alpha + pv[:, :dh]
        return carry

    lax.fori_loop(0, i, body, 0)
    for h in range(nh):
        o_ref[:, h * dh:(h + 1) * dh] = (acc_ref[h] / l_ref[h]).astype(o_ref.dtype)


def _moba_prompt_attn(q_all, k_b, v_b, kmean, rel_table, n_batch, seq_len):
    blk = MOBA_BLOCK
    nq = seq_len // blk
    dh = MOBA_HEAD_DIM
    nh = MOBA_HEADS
    d = nh * dh
    assert nq <= LANES and (nq % 8 == 0 or n_batch == 1) and dh == LANES
    return pl.pallas_call(
        _moba_prompt_kernel,
        grid=(n_batch, nq),
        in_specs=[pl.BlockSpec((blk, d), lambda b, i: (b * nq + i, 0)),
                  pl.BlockSpec((None, nq, d, blk), lambda b, i: (b, 0, 0, 0)),
                  pl.BlockSpec((seq_len, d), lambda b, i: (b, 0)),
                  pl.BlockSpec((nq, d), lambda b, i: (b, 0)),
                  pl.BlockSpec(memory_space=pltpu.SMEM)],
        out_specs=pl.BlockSpec((blk, d), lambda b, i: (b * nq + i, 0)),
        out_shape=jax.ShapeDtypeStruct((n_batch * seq_len, d), BF16),
        scratch_shapes=[pltpu.VMEM((nh, blk, blk), F32), pltpu.VMEM((nh, blk, blk), F32),
                        pltpu.VMEM((nh, blk, dh + LANES), BF16), pltpu.VMEM((nh, blk, dh), F32),
                        pltpu.VMEM((nh, blk, LANES), F32), pltpu.VMEM((nh, blk, LANES), F32)],
        compiler_params=_cparams(("arbitrary", "arbitrary")),
        name="moba_prompt_attn",
    )(q_all, k_b, v_b, kmean, rel_table)


def _moba_sample_kernel(layer, n_pages, page, t_new, cpages,
                        pt_ref, q_ref, kn_ref, vn_ref, trow_ref, k_hbm, v_hbm, o_ref,
                        buf, sum_ref, s_buf, sem):
    b = pl.program_id(0)
    nb = pl.num_programs(0)
    nh, dh, blk = MOBA_HEADS, MOBA_HEAD_DIM, MOBA_BLOCK
    rows = nh * t_new
    past = n_pages * page
    chunk = cpages * page
    n_chunks = n_pages // cpages
    nblk = past // blk
    bpc = chunk // blk
    scale = dh ** -0.5

    def start_chunk(seq, ci, src, sl):
        def body(j, carry):
            pid = pt_ref[seq, ci * cpages + j]
            dst_rows = pl.ds(pl.multiple_of(j * page, page), page)
            for h in range(nh):
                pltpu.make_async_copy(src.at[layer, pid, :, h, :], buf.at[sl, h, dst_rows, :], sem.at[sl]).start()
            return carry
        lax.fori_loop(0, cpages, body, 0)

    def wait_chunk(src, sl):
        for _ in range(cpages * nh):
            pltpu.make_async_copy(src.at[layer, 0, :, 0, :], buf.at[sl, 0, pl.ds(0, page), :], sem.at[sl]).wait()

    @pl.when(b == 0)
    def _():
        start_chunk(0, 0, k_hbm, 0)

    q = q_ref[...]
    qh = [q[:, h * dh:(h + 1) * dh] for h in range(nh)]
    qh_b = [x.astype(BF16) for x in qh]

    for ci in range(n_chunks):
        sl = ci % 2
        if ci + 1 < n_chunks:
            start_chunk(b, ci + 1, k_hbm, 1 - sl)
        else:
            start_chunk(b, 0, v_hbm, 1 - sl)
        wait_chunk(k_hbm, sl)
        for h in range(nh):
            kf = buf[sl, h]
            sum_ref[h, ci * bpc:(ci + 1) * bpc, :] = jnp.sum(kf.reshape(bpc, blk, dh), axis=1)
            s_buf[h * t_new:(h + 1) * t_new, ci * chunk:(ci + 1) * chunk] = _dot_nt(qh_b[h], kf.astype(BF16)) * scale

    gate = jnp.concatenate(
        [_dot_nt_f32(qh[h], sum_ref[h] * (1.0 / blk)) for h in range(nh)], axis=0)
    bl = lax.broadcasted_iota(I32, gate.shape, 1)
    big = jnp.int32(1 << 20)
    sel = jnp.zeros(gate.shape, F32)
    g = gate
    for _ in range(min(MOBA_TOPK, nblk)):
        gm = jnp.max(g, axis=-1, keepdims=True)
        gi = jnp.min(jnp.where(g == gm, bl, big), axis=-1, keepdims=True)
        hit = bl == gi
        sel = jnp.where(hit, 1.0, sel)
        g = jnp.where(hit, NEG * 2.0, g)
    kpos_blk = lax.broadcasted_iota(I32, (nblk, past), 1) // blk
    expand = jnp.where(kpos_blk == lax.broadcasted_iota(I32, (nblk, past), 0), 1.0, 0.0).astype(BF16)
    keep = _dot(sel.astype(BF16), expand)

    trow = trow_ref[...]
    col = lambda b_: trow[:, b_:b_ + 1]
    t_of_row = lax.broadcasted_iota(I32, (rows, LANES), 0) & (t_new - 1)
    kpos_tail = (past - LANES) + lax.broadcasted_iota(I32, (rows, LANES), 1)
    bias_tail = _t5_bias(past + t_of_row - kpos_tail, col)
    s = s_buf[...] + col(REL_BUCKETS - 1)
    s_tail = s_buf[:, past - LANES:] + bias_tail
    s = jnp.concatenate([s[:, :past - LANES], s_tail], axis=-1)
    s = jnp.where(keep > 0.5, s, NEG)

    kn = kn_ref[...].astype(BF16)
    vn = vn_ref[...].astype(BF16)
    s_own = jnp.concatenate([_dot_nt(qh_b[h], kn[:, h * dh:(h + 1) * dh]) for h in range(nh)], axis=0) * scale
    r_t = lax.broadcasted_iota(I32, s_own.shape, 0) & (t_new - 1)
    c_t = lax.broadcasted_iota(I32, s_own.shape, 1)
    s_own = s_own + _t5_bias(jnp.maximum(r_t - c_t, 0), col)
    s_own = jnp.where(c_t <= r_t, s_own, NEG)

    m = jnp.maximum(jnp.max(s, axis=-1, keepdims=True), jnp.max(s_own, axis=-1, keepdims=True))
    p = jnp.exp(s - m)
    p_own = jnp.exp(s_own - m)
    l = jnp.sum(p, axis=-1, keepdims=True) + jnp.sum(p_own, axis=-1, keepdims=True)
    s_buf[...] = p
    p_own = p_own.astype(BF16)
    acc = [_dot(p_own[h * t_new:(h + 1) * t_new, :], vn[:, h * dh:(h + 1) * dh]) for h in range(nh)]

    for ci in range(n_chunks):
        sl = (n_chunks + ci) % 2
        if ci + 1 < n_chunks:
            start_chunk(b, ci + 1, v_hbm, 1 - sl)
        else:
            @pl.when(b + 1 < nb)
            def _():
                start_chunk(b + 1, 0, k_hbm, 1 - sl)
        wait_chunk(v_hbm, sl)
        for h in range(nh):
            vh = buf[sl, h].astype(BF16)
            ph = s_buf[h * t_new:(h + 1) * t_new, ci * chunk:(ci + 1) * chunk].astype(BF16)
            acc[h] = acc[h] + _dot(ph, vh)
    for h in range(nh):
        o_ref[:, h * dh:(h + 1) * dh] = acc[h] / l[h * t_new:(h + 1) * t_new, :]


def _moba_sample_attn(page_table, q_all, k_s, v_s, rel_table, cache_k, cache_v, layer, n_p, t_new):
    n_seq, n_pages = page_table.shape
    n_layers, n_phys, page, nh, dh = cache_k.shape
    d = nh * dh
    past = n_pages * page
    cpages = min(8, n_pages // 2)
    assert n_pages % cpages == 0 and (n_pages // cpages) % 2 == 0
    rows = nh * t_new
    q_off = n_p // t_new
    trow = jnp.repeat(rel_table.T, t_new, axis=0)
    grid_spec = pltpu.PrefetchScalarGridSpec(
        num_scalar_prefetch=1,
        grid=(n_seq,),
        in_specs=[pl.BlockSpec((t_new, d), lambda b, pt: (q_off + b, 0)),
                  pl.BlockSpec((t_new, d), lambda b, pt: (b, 0)),
                  pl.BlockSpec((t_new, d), lambda b, pt: (b, 0)),
                  pl.BlockSpec((rows, REL_BUCKETS), lambda b, pt: (0, 0)),
                  pl.BlockSpec(memory_space=pl.ANY),
                  pl.BlockSpec(memory_space=pl.ANY)],
        out_specs=pl.BlockSpec((t_new, d), lambda b, pt: (b, 0)),
        scratch_shapes=[pltpu.VMEM((2, nh, cpages * page, dh), F32),
                        pltpu.VMEM((nh, past // MOBA_BLOCK, dh), F32),
                        pltpu.VMEM((rows, past), F32),
                        pltpu.SemaphoreType.DMA((2,))])
    return pl.pallas_call(
        functools.partial(_moba_sample_kernel, layer, n_pages, page, t_new, cpages),
        grid_spec=grid_spec,
        out_shape=jax.ShapeDtypeStruct((n_seq * t_new, d), F32),
        compiler_params=_cparams(("arbitrary",)),
        name="moba_sample_attn",
    )(page_table, q_all, k_s, v_s, trow, cache_k, cache_v)


def _rope_tables(pos):
    half = MLA_ROPE // 2
    inv_freq = ROPE_THETA ** (-jnp.arange(half, dtype=F32) / half)
    ang = pos.astype(F32)[:, None] * inv_freq
    z = jnp.zeros((pos.shape[0], LANES - MLA_ROPE), F32)
    cos = jnp.concatenate([jnp.cos(ang), jnp.cos(ang), z], axis=-1)
    sin = jnp.concatenate([jnp.sin(ang), jnp.sin(ang), z], axis=-1)
    return cos, sin


def kernel(x_prompt, x_sample, cache_mla_ckv, cache_mla_kpe, cache_moba_k, cache_moba_v, page_table, c_prompt, c_sample, w_ada, b_ada, g_norm_mix, g_norm_ffn, g_final, w_dq, g_q, w_uq, w_dkv, g_kv, w_uk, w_uv, w_o_mla, w_qkv_moba, w_o_moba, rel_bias_table, w_router_group, w_router_expert, w_exp_gate, w_exp_up, w_exp_down):
    n_batch, seq_len, d = x_prompt.shape
    n_seq, t_new, _ = x_sample.shape
    depth = w_ada.shape[0]
    page = cache_mla_ckv.shape[2]
    past = page_table.shape[1] * page
    n_p, n_s = n_batch * seq_len, n_seq * t_new
    tile = min(TOK_TILE, n_s)
    assert n_s % tile == 0 and seq_len % tile == 0 and tile % t_new == 0

    xp = x_prompt.reshape(n_p, d)
    xs = x_sample.reshape(n_s, d)
    m = _ada(jnp.concatenate([c_sample, c_prompt], axis=0), w_ada, b_ada)
    mp4 = m[:, n_seq:, :].reshape(depth, n_batch, 1, 6 * d)
    ms3 = jnp.repeat(m[:, :n_seq, :], t_new, axis=1)

    cos_p, sin_p = _rope_tables(jnp.arange(seq_len))
    cos_s, sin_s = _rope_tables(past + (jnp.arange(tile) % t_new))
    cos_all = jnp.concatenate([cos_p, cos_s], axis=0)
    sin_all = jnp.concatenate([sin_p, sin_s], axis=0)

    outs = {}
    x = (xp, xs)
    for i in range(depth):
        if i % 2 == 0:
            la = i // 2
            if not isinstance(x, tuple):
                x = (x[:n_p], x[n_p:])
            ckv_p, ckv_s, kpe_p, kpe_s, q_all, k_all, q_s, k_s = _mla_proj(
                x[0], x[1], mp4, ms3, i, g_norm_mix[i], cos_all, sin_all,
                w_dq[la], g_q[la], w_uq[la], w_dkv[la], g_kv[la], w_uk[la], seq_len, tile)
            o_p = _mla_prompt_attn(q_all, k_all, n_batch, seq_len, min(ATT_TILE, seq_len))
            o_s = _mla_sample_attn(page_table, q_s, k_s, cache_mla_ckv, cache_mla_kpe, la, t_new)
            outs.setdefault("ckv_p", []).append(ckv_p.reshape(n_batch, seq_len, MLA_KV_RANK))
            outs.setdefault("kpe_p", []).append(kpe_p.reshape(n_batch, seq_len, MLA_ROPE))
            outs.setdefault("ckv_s", []).append(ckv_s.reshape(n_seq, t_new, MLA_KV_RANK))
            outs.setdefault("kpe_s", []).append(kpe_s.reshape(n_seq, t_new, MLA_ROPE))
            x_new, h2, route = _attn_out(o_p, o_s, x, mp4, ms3, i, g_norm_ffn[i], w_uv[la], w_o_mla[la],
                                         w_router_group[i], w_router_expert[i], seq_len, tile)
        else:
            lb = i // 2
            xu = x if not isinstance(x, tuple) else jnp.concatenate(x, axis=0)
            q_all, k_p, k_s, v_p, v_s, k_b, v_b, kmean = _moba_qkv(
                xu, mp4, ms3, i, g_norm_mix[i], w_qkv_moba[lb], n_p, seq_len, tile)
            o_p = _moba_prompt_attn(q_all, k_b, v_b, kmean, rel_bias_table, n_batch, seq_len)
            o_s = _moba_sample_attn(page_table, q_all, k_s, v_s, rel_bias_table, cache_moba_k, cache_moba_v,
                                    lb, n_p, t_new)
            hd = (MOBA_HEADS, MOBA_HEAD_DIM)
            outs.setdefault("k_p", []).append(k_p.reshape(n_batch, seq_len, *hd))
            outs.setdefault("v_p", []).append(v_p.reshape(n_batch, seq_len, *hd))
            outs.setdefault("k_s", []).append(k_s.reshape(n_seq, t_new, *hd))
            outs.setdefault("v_s", []).append(v_s.reshape(n_seq, t_new, *hd))
            x_new, h2, route = _attn_out(o_p, o_s, xu, mp4, ms3, i, g_norm_ffn[i], None, w_o_moba[lb],
                                         w_router_group[i], w_router_expert[i], seq_len, tile)
        y2 = _moe(h2, route, w_exp_gate, w_exp_up, w_exp_down, i, MOE_TILE)
        last = i == depth - 1
        x = _combine(x_new, y2, mp4, ms3, i, g_final if last else None, n_p, seq_len, tile)
    y_p, y_s = x
    st = lambda key: jnp.stack(outs[key])
    return (y_p.reshape(n_batch, seq_len, d), y_s.reshape(n_seq, t_new, d),
            st("ckv_p"), st("kpe_p"), st("k_p"), st("v_p"), st("ckv_s"), st("kpe_s"), st("k_s"), st("v_s"))
```

```python
import functools
import math

import numpy as np
import jax
import jax.numpy as jnp
from jax import lax
from jax.experimental import pallas as pl
from jax.experimental.pallas import tpu as pltpu

F32 = jnp.float32
BF16 = jnp.bfloat16
I32 = jnp.int32

EPS = 1e-6
ROPE_THETA = 10000.0
MLA_HEADS = 8
MLA_NOPE = 128
MLA_ROPE = 64
MLA_KV_RANK = 256
MLA_V = 128
MLA_SCALE = (MLA_NOPE + MLA_ROPE) ** -0.5
MLA_QK = MLA_KV_RANK + 128
MOBA_HEADS = 8
MOBA_HEAD_DIM = 128
MOBA_BLOCK = 256
MOBA_TOPK = 3
REL_BUCKETS = 32
REL_MAX_DIST = 128
N_GROUPS = 4
EXPERTS_PER_GROUP = 8
N_EXPERTS = N_GROUPS * EXPERTS_PER_GROUP
LANES = 128
NEG = -1e30

TOK_TILE = 512
ATT_TILE = 256
MOE_TILE = 256
VMEM_LIMIT = 56 * 1024 * 1024


def _t5_thresholds():
    max_exact = REL_BUCKETS // 2
    d = np.arange(1, 4 * REL_MAX_DIST).astype(np.float32)
    t = (np.log(d / np.float32(max_exact)) / np.float32(math.log(REL_MAX_DIST / max_exact))
         * np.float32(REL_BUCKETS - max_exact)).astype(np.float32)
    b = np.where(d < max_exact, d.astype(np.int32), np.minimum(max_exact + t.astype(np.int32), REL_BUCKETS - 1))
    return [int(d[np.argmax(b >= k)]) for k in range(REL_BUCKETS)]


T5_THR = _t5_thresholds()


def _cparams(sem):
    return pltpu.CompilerParams(dimension_semantics=sem, vmem_limit_bytes=VMEM_LIMIT)


def _rms(x, g):
    return x * lax.rsqrt(jnp.mean(x * x, axis=-1, keepdims=True) + EPS) * g


def _sigmoid(x):
    return 1.0 / (1.0 + jnp.exp(-x))


def _dot(a, b):
    return jnp.dot(a, b, preferred_element_type=F32)


def _dot_nt(a, b):
    return lax.dot_general(a, b, (((1,), (1,)), ((), ())), preferred_element_type=F32)


def _dot_f32(a, b):
    return jnp.dot(a, b, preferred_element_type=F32, precision=lax.Precision.HIGHEST)


def _dot_nt_f32(a, b):
    return lax.dot_general(a, b, (((1,), (1,)), ((), ())), preferred_element_type=F32,
                           precision=lax.Precision.HIGHEST)


def _split_specs(n_pt, cols, tile):
    return [pl.BlockSpec((tile, cols), lambda i: (jnp.minimum(i, n_pt - 1), 0)),
            pl.BlockSpec((tile, cols), lambda i: (jnp.maximum(i - n_pt, 0), 0))]


def _mod_specs(layer, k, n_pt, tiles_per_seq, n_batch, d, tile):
    return [pl.BlockSpec((None, None, 1, d),
                         lambda i: (layer, jnp.minimum(i // tiles_per_seq, n_batch - 1), 0, k)),
            pl.BlockSpec((None, tile, d), lambda i: (layer, jnp.maximum(i - n_pt, 0), k))]


def _full(shape):
    nd = len(shape)
    return pl.BlockSpec(shape, lambda *_: (0,) * nd)


def _ada_kernel(c_ref, w_ref, b_ref, o_ref):
    c = c_ref[...]
    s = c * _sigmoid(c)
    o_ref[...] = _dot(s.astype(BF16), w_ref[...].astype(BF16)) + b_ref[...]


def _ada(c_all, w_ada, b_ada):
    depth, d, d6 = w_ada.shape
    n = c_all.shape[0]
    cols = 1536
    return pl.pallas_call(
        _ada_kernel,
        grid=(depth, d6 // cols),
        in_specs=[pl.BlockSpec((n, d), lambda l, j: (0, 0)),
                  pl.BlockSpec((None, d, cols), lambda l, j: (l, 0, j)),
                  pl.BlockSpec((None, 1, cols), lambda l, j: (l, 0, j))],
        out_specs=pl.BlockSpec((None, n, cols), lambda l, j: (l, 0, j)),
        out_shape=jax.ShapeDtypeStruct((depth, n, d6), F32),
        compiler_params=_cparams(("arbitrary", "arbitrary")),
        name="ada_params",
    )(c_all, w_ada, b_ada.reshape(depth, 1, d6))


def _mla_proj_kernel(n_pt, xp_ref, xs_ref, shp_ref, shs_ref, scp_ref, scs_ref, gmix_ref, cos_ref, sin_ref,
                     wdq_ref, gq_ref, wuqn_ref, wuqp_ref, wuqr_ref, wuk_ref, wdkv_ref, wdkvr_ref, gkv_ref,
                     ckvp_ref, ckvs_ref, kpep_ref, kpes_ref, q_ref, k_ref, qs_ref, ks_ref):
    i = pl.program_id(0)
    is_s = i >= n_pt
    x = jnp.where(is_s, xs_ref[...], xp_ref[...])
    shift = jnp.where(is_s, shs_ref[...], shp_ref[...])
    scale = jnp.where(is_s, scs_ref[...], scp_ref[...])
    h = (_rms(x, gmix_ref[...]) * (1.0 + scale) + shift).astype(BF16)
    cos = cos_ref[...]
    sin = sin_ref[...]

    cq = _dot(h, wdq_ref[...])
    cqn = _rms(cq, gq_ref[...]).astype(BF16)
    qn = _dot(cqn, wuqn_ref[...]).astype(BF16)
    qpe = _dot(cqn, wuqp_ref[...])
    qrot = _dot(cqn, wuqr_ref[...])
    kv = _dot(h, wdkv_ref[...])
    kvrot = _dot(h, wdkvr_ref[...])
    ckv = _rms(kv[:, :MLA_KV_RANK], gkv_ref[...])
    kpe = kv[:, MLA_KV_RANK:] * cos + kvrot * sin

    pieces = []
    for hh in range(MLA_HEADS):
        sl = slice(hh * LANES, (hh + 1) * LANES)
        pieces.append(_dot(qn[:, sl], wuk_ref[hh]))
        pieces.append(qpe[:, sl] * cos + qrot[:, sl] * sin)
    q = jnp.concatenate(pieces, axis=-1)
    k = jnp.concatenate([ckv, kpe], axis=-1)
    q_ref[...] = q.astype(BF16)
    k_ref[...] = k.astype(BF16)

    @pl.when(jnp.logical_not(is_s))
    def _():
        ckvp_ref[...] = ckv
        kpep_ref[...] = kpe[:, :MLA_ROPE]

    @pl.when(is_s)
    def _():
        ckvs_ref[...] = ckv
        kpes_ref[...] = kpe[:, :MLA_ROPE]
        qs_ref[...] = q
        ks_ref[...] = k


def _rot_cols(w):
    half = MLA_ROPE // 2
    return jnp.concatenate([-w[..., half:], w[..., :half]], axis=-1)


def _mla_proj(xp, xs, mp4, ms3, layer, g_mix, cos_all, sin_all, w_dq, g_q, w_uq, w_dkv, g_kv, w_uk,
              seq_len, tile):
    n_p, d = xp.shape
    n_s = xs.shape[0]
    n_pt, n_st = n_p // tile, n_s // tile
    tps = seq_len // tile
    n_batch = n_p // seq_len
    qr = w_dq.shape[1]
    hq = MLA_HEADS * LANES
    wq = w_uq.reshape(qr, MLA_HEADS, MLA_NOPE + MLA_ROPE)
    w_n = wq[:, :, :MLA_NOPE].reshape(qr, hq).astype(BF16)
    zpad = jnp.zeros((qr, MLA_HEADS, LANES - MLA_ROPE), F32)
    w_p = jnp.concatenate([wq[:, :, MLA_NOPE:], zpad], axis=-1).reshape(qr, hq).astype(BF16)
    w_r = jnp.concatenate([_rot_cols(wq[:, :, MLA_NOPE:]), zpad], axis=-1).reshape(qr, hq).astype(BF16)
    w_ukt = jnp.transpose(w_uk, (1, 2, 0)).astype(BF16)
    zk = jnp.zeros((d, LANES - MLA_ROPE), F32)
    w_kv = jnp.concatenate([w_dkv, zk], axis=-1).astype(BF16)
    w_kvr = jnp.concatenate([_rot_cols(w_dkv[:, MLA_KV_RANK:]), zk], axis=-1).astype(BF16)
    n_tab = cos_all.shape[0] // tile - 1

    tab_spec = pl.BlockSpec((tile, LANES), lambda i: (jnp.where(i < n_pt, i % tps, n_tab), 0))
    in_specs = (_split_specs(n_pt, d, tile)
                + _mod_specs(layer, 0, n_pt, tps, n_batch, d, tile)
                + _mod_specs(layer, 1, n_pt, tps, n_batch, d, tile)
                + [_full((1, d)), tab_spec, tab_spec,
                   _full((d, qr)), _full((1, qr)), _full((qr, hq)), _full((qr, hq)), _full((qr, hq)),
                   _full((MLA_HEADS, MLA_NOPE, MLA_KV_RANK)), _full((d, MLA_QK)), _full((d, LANES)),
                   _full((1, MLA_KV_RANK))])
    n_all = n_p + n_s
    out_shape = (jax.ShapeDtypeStruct((n_p, MLA_KV_RANK), F32), jax.ShapeDtypeStruct((n_s, MLA_KV_RANK), F32),
                 jax.ShapeDtypeStruct((n_p, MLA_ROPE), F32), jax.ShapeDtypeStruct((n_s, MLA_ROPE), F32),
                 jax.ShapeDtypeStruct((n_all, MLA_HEADS * MLA_QK), BF16),
                 jax.ShapeDtypeStruct((n_all, MLA_QK), BF16),
                 jax.ShapeDtypeStruct((n_s, MLA_HEADS * MLA_QK), F32),
                 jax.ShapeDtypeStruct((n_s, MLA_QK), F32))
    out_specs = (_split_specs(n_pt, MLA_KV_RANK, tile) + _split_specs(n_pt, MLA_ROPE, tile)
                 + [pl.BlockSpec((tile, MLA_HEADS * MLA_QK), lambda i: (i, 0)),
                    pl.BlockSpec((tile, MLA_QK), lambda i: (i, 0)),
                    _split_specs(n_pt, MLA_HEADS * MLA_QK, tile)[1],
                    _split_specs(n_pt, MLA_QK, tile)[1]])
    return pl.pallas_call(
        functools.partial(_mla_proj_kernel, n_pt),
        grid=(n_pt + n_st,),
        in_specs=in_specs, out_specs=out_specs, out_shape=out_shape,
        compiler_params=_cparams(("arbitrary",)),
        name="mla_proj",
    )(xp, xs, mp4, ms3, mp4, ms3, g_mix.reshape(1, d), cos_all, sin_all,
      w_dq.astype(BF16), g_q.reshape(1, qr), w_n, w_p, w_r, w_ukt, w_kv, w_kvr, g_kv.reshape(1, MLA_KV_RANK))


def _mla_prompt_kernel(tq, q_ref, k_ref, o_ref, acc_ref):
    i = pl.program_id(1)
    m_rows = MLA_HEADS * tq
    q = q_ref[...]
    qs = jnp.concatenate([q[:, h * MLA_QK:(h + 1) * MLA_QK] for h in range(MLA_HEADS)], axis=0)

    def step(kblk, m, l, masked):
        s = _dot_nt(qs, kblk) * MLA_SCALE
        if masked:
            r = lax.broadcasted_iota(I32, s.shape, 0) & (tq - 1)
            c = lax.broadcasted_iota(I32, s.shape, 1)
            s = jnp.where(c <= r, s, NEG)
        m_new = jnp.maximum(m, jnp.max(s, axis=-1, keepdims=True))
        alpha = jnp.exp(m - m_new)
        p = jnp.exp(s - m_new)
        l_new = alpha * l + jnp.sum(p, axis=-1, keepdims=True)
        pv = _dot(p.astype(BF16), kblk[:, :MLA_KV_RANK])
        return m_new, l_new, alpha, pv

    kd = k_ref[pl.ds(pl.multiple_of(i * tq, tq), tq), :]
    m0 = jnp.full((m_rows, 1), NEG, F32)
    l0 = jnp.zeros((m_rows, 1), F32)
    m, l, _, pv = step(kd, m0, l0, True)
    acc_ref[...] = pv

    def body(j, carry):
        m, l = carry
        kb = k_ref[pl.ds(pl.multiple_of(j * tq, tq), tq), :]
        m, l, alpha, pv = step(kb, m, l, False)
        acc_ref[...] = acc_ref[...] * alpha + pv
        return m, l

    m, l = lax.fori_loop(0, i, body, (m, l))
    o = acc_ref[...] / l
    for h in range(MLA_HEADS):
        o_ref[:, h * MLA_KV_RANK:(h + 1) * MLA_KV_RANK] = o[h * tq:(h + 1) * tq, :].astype(o_ref.dtype)


def _mla_prompt_attn(q_all, k_all, n_batch, seq_len, tq):
    nq = seq_len // tq
    return pl.pallas_call(
        functools.partial(_mla_prompt_kernel, tq),
        grid=(n_batch, nq),
        in_specs=[pl.BlockSpec((tq, MLA_HEADS * MLA_QK), lambda b, i: (b * nq + i, 0)),
                  pl.BlockSpec((seq_len, MLA_QK), lambda b, i: (b, 0))],
        out_specs=pl.BlockSpec((tq, MLA_HEADS * MLA_KV_RANK), lambda b, i: (b * nq + i, 0)),
        out_shape=jax.ShapeDtypeStruct((n_batch * seq_len, MLA_HEADS * MLA_KV_RANK), BF16),
        scratch_shapes=[pltpu.VMEM((MLA_HEADS * tq, MLA_KV_RANK), F32)],
        compiler_params=_cparams(("arbitrary", "arbitrary")),
        name="mla_prompt_attn",
    )(q_all, k_all)


def _mla_sample_kernel(layer, n_pages, page, t_new, chunk,
                       pt_ref, q_ref, k_ref, ckv_hbm, kpe_hbm, o_ref,
                       ckv_buf, kpe_buf, kb_buf, s_buf, sem):
    b = pl.program_id(0)
    nb = pl.num_programs(0)
    slot = b % 2
    past = n_pages * page

    def copies(seq, sl):
        out = []
        for j in range(n_pages):
            pid = pt_ref[seq, j]
            out.append(pltpu.make_async_copy(ckv_hbm.at[layer, pid], ckv_buf.at[sl, pl.ds(j * page, page)], sem.at[0, sl]))
            out.append(pltpu.make_async_copy(kpe_hbm.at[layer, pid], kpe_buf.at[sl, j], sem.at[1, sl]))
        return out

    @pl.when(b == 0)
    def _():
        for cp in copies(0, 0):
            cp.start()

    @pl.when(b + 1 < nb)
    def _():
        for cp in copies(b + 1, 1 - slot):
            cp.start()

    for cp in copies(b, slot):
        cp.wait()

    q = q_ref[...]
    qs = jnp.concatenate([q[:, h * MLA_QK:(h + 1) * MLA_QK] for h in range(MLA_HEADS)], axis=0).astype(BF16)
    ql = qs[:, :MLA_KV_RANK]
    qp = qs[:, MLA_KV_RANK:MLA_KV_RANK + MLA_ROPE]
    ppc = chunk // page
    for c in range(past // chunk):
        kc = ckv_buf[slot, c * chunk:(c + 1) * chunk, :].astype(BF16)
        kb_buf[c * chunk:(c + 1) * chunk, :] = kc
        s_pe = jnp.concatenate([_dot(qp, kpe_buf[slot, c * ppc + j].astype(BF16)) for j in range(ppc)], axis=-1)
        s_buf[:, c * chunk:(c + 1) * chunk] = (_dot_nt(ql, kc) + s_pe) * MLA_SCALE

    kn = k_ref[...].astype(BF16)
    s_new = _dot_nt(qs, kn) * MLA_SCALE
    r = lax.broadcasted_iota(I32, s_new.shape, 0) & (t_new - 1)
    c_ = lax.broadcasted_iota(I32, s_new.shape, 1)
    s_new = jnp.where(c_ <= r, s_new, NEG)
    s = s_buf[...]
    m = jnp.maximum(jnp.max(s, axis=-1, keepdims=True), jnp.max(s_new, axis=-1, keepdims=True))
    p_new = jnp.exp(s_new - m)
    l = jnp.sum(p_new, axis=-1, keepdims=True)
    acc = _dot(p_new.astype(BF16), kn[:, :MLA_KV_RANK])
    for c in range(past // chunk):
        p = jnp.exp(s_buf[:, c * chunk:(c + 1) * chunk] - m)
        l = l + jnp.sum(p, axis=-1, keepdims=True)
        acc = acc + _dot(p.astype(BF16), kb_buf[c * chunk:(c + 1) * chunk, :])
    o = acc / l
    for h in range(MLA_HEADS):
        o_ref[:, h * MLA_KV_RANK:(h + 1) * MLA_KV_RANK] = o[h * t_new:(h + 1) * t_new, :]


def _mla_sample_attn(page_table, q_s, k_s, cache_ckv, cache_kpe, layer, t_new):
    n_seq, n_pages = page_table.shape
    page = cache_ckv.shape[2]
    past = n_pages * page
    chunk = min(1024, past)
    rows = MLA_HEADS * t_new
    cache_kpe = jnp.swapaxes(cache_kpe, 2, 3)
    grid_spec = pltpu.PrefetchScalarGridSpec(
        num_scalar_prefetch=1,
        grid=(n_seq,),
        in_specs=[pl.BlockSpec((t_new, MLA_HEADS * MLA_QK), lambda b, pt: (b, 0)),
                  pl.BlockSpec((t_new, MLA_QK), lambda b, pt: (b, 0)),
                  pl.BlockSpec(memory_space=pl.ANY),
                  pl.BlockSpec(memory_space=pl.ANY)],
        out_specs=pl.BlockSpec((t_new, MLA_HEADS * MLA_KV_RANK), lambda b, pt: (b, 0)),
        scratch_shapes=[pltpu.VMEM((2, past, MLA_KV_RANK), F32),
                        pltpu.VMEM((2, n_pages, MLA_ROPE, page), F32),
                        pltpu.VMEM((past, MLA_KV_RANK), BF16),
                        pltpu.VMEM((rows, past), F32),
                        pltpu.SemaphoreType.DMA((2, 2))])
    return pl.pallas_call(
        functools.partial(_mla_sample_kernel, layer, n_pages, page, t_new, chunk),
        grid_spec=grid_spec,
        out_shape=jax.ShapeDtypeStruct((n_seq * t_new, MLA_HEADS * MLA_KV_RANK), F32),
        compiler_params=_cparams(("arbitrary",)),
        name="mla_sample_attn",
    )(page_table, q_s, k_s, cache_ckv, cache_kpe)


def _route(z):
    lane = lax.broadcasted_iota(I32, z.shape, 1)
    big = jnp.int32(1 << 20)
    gmask = lane < N_GROUPS
    zg = jnp.where(gmask, z, NEG)
    gmax = jnp.max(zg, axis=-1, keepdims=True)
    gidx = jnp.min(jnp.where(jnp.logical_and(gmask, zg == gmax), lane, big), axis=-1, keepdims=True)
    gsum = jnp.sum(jnp.where(gmask, jnp.exp(zg - gmax), 0.0), axis=-1, keepdims=True)
    g_top = 1.0 / gsum
    lo = N_GROUPS + gidx * EXPERTS_PER_GROUP
    emask = jnp.logical_and(lane >= lo, lane < lo + EXPERTS_PER_GROUP)
    z1 = jnp.where(emask, z, NEG)
    e1 = jnp.max(z1, axis=-1, keepdims=True)
    i1 = jnp.min(jnp.where(jnp.logical_and(emask, z1 == e1), lane, big), axis=-1, keepdims=True)
    z2 = jnp.where(lane == i1, NEG, z1)
    e2 = jnp.max(z2, axis=-1, keepdims=True)
    i2 = jnp.min(jnp.where(jnp.logical_and(emask, jnp.logical_and(z2 == e2, lane != i1)), lane, big),
                 axis=-1, keepdims=True)
    t = jnp.exp(e2 - e1)
    w1 = g_top / (1.0 + t)
    w2 = g_top * t / (1.0 + t)
    out = jnp.where(lane == 0, (i1 - N_GROUPS).astype(F32),
                    jnp.where(lane == 1, (i2 - N_GROUPS).astype(F32),
                              jnp.where(lane == 2, w1, jnp.where(lane == 3, w2, 0.0))))
    return out


def _attn_out_kernel(n_pt, split_x, use_uv, *refs):
    refs = list(refs)
    op_ref, os_ref = refs[:2]
    refs = refs[2:]
    if split_x:
        xp_ref, xs_ref = refs[:2]
        refs = refs[2:]
    else:
        x_ref = refs[0]
        refs = refs[1:]
    (gp_ref, gs_ref, shp_ref, shs_ref, scp_ref, scs_ref, gffn_ref) = refs[:7]
    refs = refs[7:]
    if use_uv:
        wuv_ref = refs[0]
        refs = refs[1:]
    wo_ref, wrh_ref, wrl_ref, xo_ref, h2_ref, rt_ref = refs

    i = pl.program_id(0)
    is_s = i >= n_pt
    o = jnp.where(is_s, os_ref[...].astype(BF16), op_ref[...])
    if split_x:
        x = jnp.where(is_s, xs_ref[...], xp_ref[...])
    else:
        x = x_ref[...]
    gate = jnp.where(is_s, gs_ref[...], gp_ref[...])
    shift = jnp.where(is_s, shs_ref[...], shp_ref[...])
    scale = jnp.where(is_s, scs_ref[...], scp_ref[...])
    if use_uv:
        o = jnp.concatenate(
            [_dot(o[:, h * MLA_KV_RANK:(h + 1) * MLA_KV_RANK], wuv_ref[h]) for h in range(MLA_HEADS)],
            axis=-1).astype(BF16)
    a = _dot(o, wo_ref[...])
    xn = x + gate * a
    xo_ref[...] = xn
    h2 = _rms(xn, gffn_ref[...]) * (1.0 + scale) + shift
    h2_ref[...] = h2
    h2h = h2.astype(BF16)
    h2l = (h2 - h2h.astype(F32)).astype(BF16)
    z = _dot(h2h, wrh_ref[...]) + (_dot(h2h, wrl_ref[...]) + _dot(h2l, wrh_ref[...]))
    rt_ref[...] = _route(z)


def _attn_out(o_p, o_s, x, mp4, ms3, layer, g_ffn, w_uv, w_o, w_rg, w_re, seq_len, tile):
    split_x = isinstance(x, tuple)
    n_p = o_p.shape[0]
    n_s = o_s.shape[0]
    din = o_p.shape[1]
    d = w_o.shape[1]
    n_pt, n_st = n_p // tile, n_s // tile
    tps = seq_len // tile
    n_batch = n_p // seq_len
    use_uv = w_uv is not None
    wr = jnp.concatenate([w_rg, w_re, jnp.zeros((d, LANES - N_GROUPS - N_EXPERTS), F32)], axis=-1)
    in_specs = _split_specs(n_pt, din, tile)
    args = [o_p, o_s]
    if split_x:
        in_specs += _split_specs(n_pt, d, tile)
        args += list(x)
    else:
        in_specs += [pl.BlockSpec((tile, d), lambda i: (i, 0))]
        args += [x]
    for k in (2, 3, 4):
        in_specs += _mod_specs(layer, k, n_pt, tps, n_batch, d, tile)
        args += [mp4, ms3]
    in_specs += [_full((1, d))]
    args += [g_ffn.reshape(1, d)]
    if use_uv:
        in_specs += [_full((MLA_HEADS, MLA_KV_RANK, MLA_V))]
        args += [jnp.transpose(w_uv, (1, 0, 2)).astype(BF16)]
    wr_hi = wr.astype(BF16)
    wr_lo = (wr - wr_hi.astype(F32)).astype(BF16)
    in_specs += [_full(w_o.shape), _full((d, LANES)), _full((d, LANES))]
    args += [w_o.astype(BF16), wr_hi, wr_lo]
    n_all = n_p + n_s
    tok = lambda cols: pl.BlockSpec((tile, cols), lambda i: (i, 0))
    return pl.pallas_call(
        functools.partial(_attn_out_kernel, n_pt, split_x, use_uv),
        grid=(n_pt + n_st,),
        in_specs=in_specs,
        out_specs=[tok(d), tok(d), tok(LANES)],
        out_shape=(jax.ShapeDtypeStruct((n_all, d), F32), jax.ShapeDtypeStruct((n_all, d), F32),
                   jax.ShapeDtypeStruct((n_all, LANES), F32)),
        compiler_params=_cparams(("arbitrary",)),
        name="attn_out_router",
    )(*args)


def _moe_kernel(tm, te_ref, tr_ref, src_ref, srcn_ref, dst_ref, w_ref, h_hbm, wg_ref, wu_ref, wd_ref,
                y_hbm, xbuf, ybuf, gsem, ssem):
    t = pl.program_id(0)
    nt = pl.num_programs(0)
    slot = t % 2
    rows = tr_ref[t]
    rows_next = tr_ref[jnp.minimum(t + 1, nt - 1)]
    rows_prev = tr_ref[jnp.maximum(t - 1, 0)]

    def gather_copy(idx_ref, r, sl):
        return pltpu.make_async_copy(h_hbm.at[pl.ds(idx_ref[0, 0, r], 1)], xbuf.at[sl, pl.ds(r, 1)], gsem.at[sl])

    def scatter_copy(r):
        return pltpu.make_async_copy(ybuf.at[pl.ds(r, 1)], y_hbm.at[pl.ds(dst_ref[0, 0, r], 1)], ssem.at[0])

    @pl.when(jnp.logical_and(t == 0, rows > 0))
    def _():
        for r in range(tm):
            gather_copy(src_ref, r, 0).start()

    @pl.when(jnp.logical_and(t + 1 < nt, rows_next > 0))
    def _():
        for r in range(tm):
            gather_copy(srcn_ref, r, 1 - slot).start()

    @pl.when(jnp.logical_and(t > 0, rows_prev > 0))
    def _():
        for r in range(tm):
            scatter_copy(r).wait()

    @pl.when(rows > 0)
    def _():
        for r in range(tm):
            gather_copy(src_ref, r, slot).wait()
        xb = xbuf[slot].astype(BF16)
        a = _dot(xb, wg_ref[...].astype(BF16))
        u = _dot(xb, wu_ref[...].astype(BF16))
        act = (a * _sigmoid(a)) * u * w_ref[...]
        ybuf[...] = _dot(act.astype(BF16), wd_ref[...].astype(BF16))
        for r in range(tm):
            scatter_copy(r).start()

    @pl.when(t == nt - 1)
    def _():
        @pl.when(rows > 0)
        def _():
            for r in range(tm):
                scatter_copy(r).wait()
        ybuf[...] = jnp.zeros(ybuf.shape, ybuf.dtype)
        fill = pltpu.make_async_copy(ybuf, y_hbm.at[pl.ds(y_hbm.shape[0] - tm, tm)], ssem.at[0])
        fill.start()
        fill.wait()


def _moe(h2, route, w_gate, w_up, w_down, layer, tm):
    n, d = h2.shape
    _, n_exp, _, f = w_gate.shape
    eid = route[:, :2].astype(I32)
    ew = route[:, 2:4]
    n2 = 2 * n
    flat = eid.reshape(n2)
    order = jnp.argsort(flat, stable=True).astype(I32)
    counts = jnp.sum((flat[:, None] == jnp.arange(n_exp, dtype=I32)[None, :]).astype(I32), axis=0)
    padded = ((counts + tm - 1) // tm) * tm
    pend = jnp.cumsum(padded)
    pstart = pend - padded
    ustart = jnp.cumsum(counts) - counts
    n_tiles = n2 // tm + n_exp
    tstart = jnp.arange(n_tiles, dtype=I32) * tm
    te = jnp.minimum(jnp.sum((pend[None, :] <= tstart[:, None]).astype(I32), axis=1), n_exp - 1)
    r0 = tstart - pstart[te]
    tr = jnp.clip(counts[te] - r0, 0, tm).astype(I32)
    lane = jnp.arange(tm, dtype=I32)[None, :]
    pos = (ustart[te] + r0)[:, None] + lane
    fidx = order[jnp.clip(pos, 0, n2 - 1)]
    tok = fidx >> 1
    src = tok.reshape(n_tiles, 1, tm)
    dst = jnp.where(lane < tr[:, None], (fidx & 1) * n + tok, n2 + lane).reshape(n_tiles, 1, tm)
    wslot = ew.reshape(n2)[fidx].reshape(n_tiles, tm, 1)

    smem_blk = lambda f_: pl.BlockSpec((1, 1, tm), f_, memory_space=pltpu.SMEM)
    wspec = lambda a_, b_: pl.BlockSpec((None, None, a_, b_), lambda t, te_, tr_: (layer, te_[t], 0, 0))
    grid_spec = pltpu.PrefetchScalarGridSpec(
        num_scalar_prefetch=2,
        grid=(n_tiles,),
        in_specs=[smem_blk(lambda t, te_, tr_: (t, 0, 0)),
                  smem_blk(lambda t, te_, tr_: (jnp.minimum(t + 1, n_tiles - 1), 0, 0)),
                  smem_blk(lambda t, te_, tr_: (t, 0, 0)),
                  pl.BlockSpec((None, tm, 1), lambda t, te_, tr_: (t, 0, 0)),
                  pl.BlockSpec(memory_space=pl.ANY),
                  wspec(d, f), wspec(d, f), wspec(f, d)],
        out_specs=pl.BlockSpec(memory_space=pl.ANY),
        scratch_shapes=[pltpu.VMEM((2, tm, d), F32), pltpu.VMEM((tm, d), F32),
                        pltpu.SemaphoreType.DMA((2,)), pltpu.SemaphoreType.DMA((1,))])
    return pl.pallas_call(
        functools.partial(_moe_kernel, tm),
        grid_spec=grid_spec,
        out_shape=jax.ShapeDtypeStruct((n2 + tm, d), F32),
        compiler_params=_cparams(("arbitrary",)),
        name="moe_experts",
    )(te, tr, src, src, dst, wslot, h2, w_gate, w_up, w_down)


def _combine_kernel(n_pt, final, d, *refs):
    if final:
        x_ref, y0_ref, y1_ref, gp_ref, gs_ref, gf_ref, op_ref, os_ref = refs
    else:
        x_ref, y0_ref, y1_ref, gp_ref, gs_ref, o_ref = refs
    i = pl.program_id(0)
    is_s = i >= n_pt
    gate = jnp.where(is_s, gs_ref[...], gp_ref[...])
    xn = x_ref[...] + gate * (y0_ref[...] + y1_ref[...])
    if final:
        out = _rms(xn, gf_ref[...])

        @pl.when(jnp.logical_not(is_s))
        def _():
            op_ref[...] = out

        @pl.when(is_s)
        def _():
            os_ref[...] = out
    else:
        o_ref[...] = xn


def _combine(x, y2, mp4, ms3, layer, g_final, n_p, seq_len, tile):
    n_all, d = x.shape
    n_s = n_all - n_p
    n_pt, n_st = n_p // tile, n_s // tile
    tps = seq_len // tile
    n_batch = n_p // seq_len
    final = g_final is not None
    tok = lambda cols: pl.BlockSpec((tile, cols), lambda i: (i, 0))
    n_tt = n_all // tile
    in_specs = ([tok(d), tok(d), pl.BlockSpec((tile, d), lambda i: (n_tt + i, 0))]
                + _mod_specs(layer, 5, n_pt, tps, n_batch, d, tile))
    args = [x, y2, y2, mp4, ms3]
    if final:
        in_specs += [_full((1, d))]
        args += [g_final.reshape(1, d)]
        out_specs = _split_specs(n_pt, d, tile)
        out_shape = (jax.ShapeDtypeStruct((n_p, d), F32), jax.ShapeDtypeStruct((n_s, d), F32))
    else:
        out_specs = tok(d)
        out_shape = jax.ShapeDtypeStruct((n_all, d), F32)
    return pl.pallas_call(
        functools.partial(_combine_kernel, n_pt, final, d),
        grid=(n_pt + n_st,),
        in_specs=in_specs, out_specs=out_specs, out_shape=out_shape,
        compiler_params=_cparams(("arbitrary",)),
        name="moe_combine",
    )(*args)


def _moba_qkv_kernel(n_pt, d, x_ref, shp_ref, shs_ref, scp_ref, scs_ref, gmix_ref, w_ref,
                     q_ref, kp_ref, ks_ref, vp_ref, vs_ref, kb_ref, vb_ref, ksum_ref):
    i = pl.program_id(0)
    is_s = i >= n_pt
    shift = jnp.where(is_s, shs_ref[...], shp_ref[...])
    scale = jnp.where(is_s, scs_ref[...], scp_ref[...])
    h = (_rms(x_ref[...], gmix_ref[...]) * (1.0 + scale) + shift).astype(BF16)
    qkv = _dot(h, w_ref[...])
    q_ref[...] = qkv[:, :d]

    @pl.when(jnp.logical_not(is_s))
    def _():
        k = qkv[:, d:2 * d]
        v = qkv[:, 2 * d:]
        kp_ref[...] = k
        vp_ref[...] = v
        vb_ref[...] = v.astype(BF16)
        nb = k.shape[0] // MOBA_BLOCK
        for n in range(nb):
            kb_ref[n] = jnp.transpose(k[n * MOBA_BLOCK:(n + 1) * MOBA_BLOCK, :]).astype(BF16)
        means = [jnp.sum(k[n * MOBA_BLOCK:(n + 1) * MOBA_BLOCK, :], axis=0, keepdims=True) * (1.0 / MOBA_BLOCK)
                 for n in range(nb)]
        ksum_ref[...] = jnp.concatenate(means + [jnp.zeros((8 - nb, d), F32)], axis=0)

    @pl.when(is_s)
    def _():
        ks_ref[...] = qkv[:, d:2 * d]
        vs_ref[...] = qkv[:, 2 * d:]


def _moba_qkv(x, mp4, ms3, layer, g_mix, w_qkv, n_p, seq_len, tile):
    n_all, d = x.shape
    n_s = n_all - n_p
    n_pt, n_st = n_p // tile, n_s // tile
    tps = seq_len // tile
    n_batch = n_p // seq_len
    tok = lambda cols: pl.BlockSpec((tile, cols), lambda i: (i, 0))
    in_specs = ([tok(d)] + _mod_specs(layer, 0, n_pt, tps, n_batch, d, tile)
                + _mod_specs(layer, 1, n_pt, tps, n_batch, d, tile) + [_full((1, d)), _full((d, 3 * d))])
    sp = _split_specs(n_pt, d, tile)
    assert tile % MOBA_BLOCK == 0 and tile // MOBA_BLOCK <= 8
    nb = tile // MOBA_BLOCK
    last = lambda i: jnp.minimum(i, n_pt - 1)
    outs = pl.pallas_call(
        functools.partial(_moba_qkv_kernel, n_pt, d),
        grid=(n_pt + n_st,),
        in_specs=in_specs,
        out_specs=[tok(d), sp[0], sp[1], sp[0], sp[1],
                   pl.BlockSpec((None, nb, d, MOBA_BLOCK), lambda i: (last(i) // tps, last(i) % tps, 0, 0)),
                   sp[0],
                   pl.BlockSpec((8, d), lambda i: (last(i), 0))],
        out_shape=(jax.ShapeDtypeStruct((n_all, d), F32),
                   jax.ShapeDtypeStruct((n_p, d), F32), jax.ShapeDtypeStruct((n_s, d), F32),
                   jax.ShapeDtypeStruct((n_p, d), F32), jax.ShapeDtypeStruct((n_s, d), F32),
                   jax.ShapeDtypeStruct((n_batch, seq_len // MOBA_BLOCK, d, MOBA_BLOCK), BF16),
                   jax.ShapeDtypeStruct((n_p, d), BF16),
                   jax.ShapeDtypeStruct((n_pt * 8, d), F32)),
        compiler_params=_cparams(("arbitrary",)),
        name="moba_qkv",
    )(x, mp4, ms3, mp4, ms3, g_mix.reshape(1, d), w_qkv.astype(BF16))
    q_all, k_p, k_s, v_p, v_s, k_b, v_b, ksum = outs
    kmean = ksum.reshape(n_pt, 8, d)[:, :nb, :].reshape(n_pt * nb, d)
    return q_all, k_p, k_s, v_p, v_s, k_b, v_b, kmean


def _t5_bias(dist, table_at):
    val = jnp.full(dist.shape, 0.0, F32) + table_at(0)
    for b in range(1, REL_BUCKETS):
        val = jnp.where(dist >= T5_THR[b], table_at(b), val)
    return val


MOBA_DROP = -32768.0


def _moba_prompt_kernel(q_ref, kt_ref, v_ref, mean_ref, tab_ref, o_ref,
                        bown_ref, bprev_ref, qa_ref, acc_ref, m_ref, l_ref):
    b = pl.program_id(0)
    i = pl.program_id(1)
    nh, dh, blk = MOBA_HEADS, MOBA_HEAD_DIM, MOBA_BLOCK
    nblk = kt_ref.shape[0]
    scale = dh ** -0.5
    ones = jnp.ones((blk, LANES), BF16)
    rep = lambda x: jnp.concatenate([x] * (blk // LANES), axis=-1)
    rowmax = lambda s: jnp.broadcast_to(jnp.max(s, axis=-1, keepdims=True), (blk, LANES))
    r = lax.broadcasted_iota(I32, (blk, blk), 0)
    c = lax.broadcasted_iota(I32, (blk, blk), 1)

    @pl.when(jnp.logical_and(b == 0, i == 0))
    def _():
        for h in range(nh):
            at = lambda bb, h=h: tab_ref[bb, h]
            bown_ref[h] = _t5_bias(jnp.maximum(r - c, 0), at)
            bprev_ref[h] = _t5_bias(blk + r - c, at)

    off = pl.multiple_of(i * blk, blk)
    bi = lax.broadcasted_iota(I32, (nblk, blk), 0)
    valid = bi < i
    for h in range(nh):
        hs = slice(h * dh, (h + 1) * dh)
        q = q_ref[:, hs]
        qb = q.astype(BF16)
        gate_t = jnp.where(valid, _dot_nt_f32(mean_ref[:, hs], q), NEG)
        rank = jnp.zeros(gate_t.shape, I32)
        for m_ in range(nblk - 1):
            gm = gate_t[m_:m_ + 1, :]
            ahead = jnp.logical_or(gm > gate_t, jnp.logical_and(gm == gate_t, m_ < bi))
            rank = rank + jnp.where(jnp.logical_and(ahead, m_ < i), 1, 0)
        drop_t = jnp.where(jnp.logical_and(valid, rank < MOBA_TOPK), 0.0, MOBA_DROP)
        drop = jnp.transpose(jnp.concatenate([drop_t, jnp.zeros((LANES - nblk, blk), F32)], axis=0))
        qa_ref[h] = jnp.concatenate([qb, drop.astype(BF16)], axis=-1)

        vo = jnp.concatenate([v_ref[pl.ds(off, blk), hs], ones], axis=-1)
        s = _dot(qb, kt_ref[i, hs, :]) * scale + bown_ref[h]
        s = jnp.where(c <= r, s, NEG)
        m = rowmax(s)
        p = jnp.exp(s - rep(m))
        pv = _dot(p.astype(BF16), vo)
        m_ref[h] = m
        l_ref[h] = pv[:, dh:]
        acc_ref[h] = pv[:, :dh]

    srow = lax.broadcasted_iota(I32, (LANES, blk), 0)

    def body(j, carry):
        offj = pl.multiple_of(j * blk, blk)
        onehot_t = jnp.where(srow == j, 1.0, 0.0).astype(BF16)
        is_prev = j == i - 1
        for h in range(nh):
            hs = slice(h * dh, (h + 1) * dh)
            kj = jnp.concatenate([kt_ref[j, hs, :], onehot_t], axis=0)
            vj = jnp.concatenate([v_ref[pl.ds(offj, blk), hs], ones], axis=-1)
            bias = jnp.where(is_prev, bprev_ref[h], tab_ref[REL_BUCKETS - 1, h])
            s = _dot(qa_ref[h], kj) * scale + bias
            m = m_ref[h]
            m_new = jnp.maximum(m, rowmax(s))
            alpha = jnp.exp(m - m_new)
            p = jnp.exp(s - rep(m_new))
            pv = _dot(p.astype(BF16), vj)
            m_ref[h] = m_new
            l_ref[h] = alpha * l_ref[h] + pv[:, dh:]
            acc_ref[h] = acc_ref[h] * alpha + pv[:, :dh]
        return carry

    lax.fori_loop(0, i, body, 0)
    for h in range(nh):
        o_ref[:, h * dh:(h + 1) * dh] = (acc_ref[h] / l_ref[h]).astype(o_ref.dtype)


def _moba_prompt_attn(q_all, k_b, v_b, kmean, rel_table, n_batch, seq_len):
    blk = MOBA_BLOCK
    nq = seq_len // blk
    dh = MOBA_HEAD_DIM
    nh = MOBA_HEADS
    d = nh * dh
    assert nq <= LANES and (nq % 8 == 0 or n_batch == 1) and dh == LANES
    return pl.pallas_call(
        _moba_prompt_kernel,
        grid=(n_batch, nq),
        in_specs=[pl.BlockSpec((blk, d), lambda b, i: (b * nq + i, 0)),
                  pl.BlockSpec((None, nq, d, blk), lambda b, i: (b, 0, 0, 0)),
                  pl.BlockSpec((seq_len, d), lambda b, i: (b, 0)),
                  pl.BlockSpec((nq, d), lambda b, i: (b, 0)),
                  pl.BlockSpec(memory_space=pltpu.SMEM)],
        out_specs=pl.BlockSpec((blk, d), lambda b, i: (b * nq + i, 0)),
        out_shape=jax.ShapeDtypeStruct((n_batch * seq_len, d), BF16),
        scratch_shapes=[pltpu.VMEM((nh, blk, blk), F32), pltpu.VMEM((nh, blk, blk), F32),
                        pltpu.VMEM((nh, blk, dh + LANES), BF16), pltpu.VMEM((nh, blk, dh), F32),
                        pltpu.VMEM((nh, blk, LANES), F32), pltpu.VMEM((nh, blk, LANES), F32)],
        compiler_params=_cparams(("arbitrary", "arbitrary")),
        name="moba_prompt_attn",
    )(q_all, k_b, v_b, kmean, rel_table)


def _moba_sample_kernel(layer, n_pages, page, t_new, cpages,
                        pt_ref, q_ref, kn_ref, vn_ref, trow_ref, k_hbm, v_hbm, o_ref,
                        buf, sum_ref, s_buf, sem):
    b = pl.program_id(0)
    nb = pl.num_programs(0)
    nh, dh, blk = MOBA_HEADS, MOBA_HEAD_DIM, MOBA_BLOCK
    rows = nh * t_new
    past = n_pages * page
    chunk = cpages * page
    n_chunks = n_pages // cpages
    nblk = past // blk
    bpc = chunk // blk
    scale = dh ** -0.5

    def start_chunk(seq, ci, src, sl):
        def body(j, carry):
            pid = pt_ref[seq, ci * cpages + j]
            dst_rows = pl.ds(pl.multiple_of(j * page, page), page)
            for h in range(nh):
                pltpu.make_async_copy(src.at[layer, pid, :, h, :], buf.at[sl, h, dst_rows, :], sem.at[sl]).start()
            return carry
        lax.fori_loop(0, cpages, body, 0)

    def wait_chunk(src, sl):
        for _ in range(cpages * nh):
            pltpu.make_async_copy(src.at[layer, 0, :, 0, :], buf.at[sl, 0, pl.ds(0, page), :], sem.at[sl]).wait()

    @pl.when(b == 0)
    def _():
        start_chunk(0, 0, k_hbm, 0)

    q = q_ref[...]
    qh = [q[:, h * dh:(h + 1) * dh] for h in range(nh)]
    qh_b = [x.astype(BF16) for x in qh]

    for ci in range(n_chunks):
        sl = ci % 2
        if ci + 1 < n_chunks:
            start_chunk(b, ci + 1, k_hbm, 1 - sl)
        wait_chunk(k_hbm, sl)
        for h in range(nh):
            kf = buf[sl, h]
            sum_ref[h, ci * bpc:(ci + 1) * bpc, :] = jnp.sum(kf.reshape(bpc, blk, dh), axis=1)
            s_buf[h * t_new:(h + 1) * t_new, ci * chunk:(ci + 1) * chunk] = _dot_nt(qh_b[h], kf.astype(BF16)) * scale

    gate = jnp.concatenate(
        [_dot_nt_f32(qh[h], sum_ref[h] * (1.0 / blk)) for h in range(nh)], axis=0)
    bl = lax.broadcasted_iota(I32, gate.shape, 1)
    big = jnp.int32(1 << 20)
    sel = jnp.zeros(gate.shape, F32)
    g = gate
    for _ in range(min(MOBA_TOPK, nblk)):
        gm = jnp.max(g, axis=-1, keepdims=True)
        gi = jnp.min(jnp.where(g == gm, bl, big), axis=-1, keepdims=True)
        hit = bl == gi
        sel = jnp.where(hit, 1.0, sel)
        g = jnp.where(hit, NEG * 2.0, g)

    lane1 = lax.broadcasted_iota(I32, (1, nblk), 1)
    pw = jnp.left_shift(1, lane1 & 15).astype(F32)
    words = []
    for h in range(nh):
        need_h = jnp.max(sel[h * t_new:(h + 1) * t_new, :], axis=0, keepdims=True) * pw
        words.append([jnp.sum(jnp.where((lane1 >> 4) == w, need_h, 0.0)).astype(I32)
                      for w in range((nblk + 15) // 16)])
    ppb = blk // page

    def v_copies(ci, sl, n, h):
        out = []
        for pg in range(ppb):
            j = n * ppb + pg
            pid = pt_ref[b, ci * cpages + j]
            out.append(pltpu.make_async_copy(v_hbm.at[layer, pid, :, h, :],
                                             buf.at[sl, h, pl.ds(j * page, page), :], sem.at[sl]))
        return out

    def for_needed(ci, fn):
        for n in range(bpc):
            g_blk = ci * bpc + n
            for h in range(nh):
                @pl.when(((words[h][g_blk // 16] >> (g_blk % 16)) & 1) == 1)
                def _(n=n, h=h):
                    fn(n, h)

    def start_v(ci, sl):
        for_needed(ci, lambda n, h: [cp.start() for cp in v_copies(ci, sl, n, h)])

    def wait_v(ci, sl):
        for_needed(ci, lambda n, h: [cp.wait() for cp in v_copies(ci, sl, n, h)])

    start_v(0, n_chunks % 2)
    kpos_blk = lax.broadcasted_iota(I32, (nblk, past), 1) // blk
    expand = jnp.where(kpos_blk == lax.broadcasted_iota(I32, (nblk, past), 0), 1.0, 0.0).astype(BF16)
    keep = _dot(sel.astype(BF16), expand)

    trow = trow_ref[...]
    col = lambda b_: trow[:, b_:b_ + 1]
    t_of_row = lax.broadcasted_iota(I32, (rows, LANES), 0) & (t_new - 1)
    kpos_tail = (past - LANES) + lax.broadcasted_iota(I32, (rows, LANES), 1)
    bias_tail = _t5_bias(past + t_of_row - kpos_tail, col)
    s = s_buf[...] + col(REL_BUCKETS - 1)
    s_tail = s_buf[:, past - LANES:] + bias_tail
    s = jnp.concatenate([s[:, :past - LANES], s_tail], axis=-1)
    s = jnp.where(keep > 0.5, s, NEG)

    kn = kn_ref[...].astype(BF16)
    vn = vn_ref[...].astype(BF16)
    s_own = jnp.concatenate([_dot_nt(qh_b[h], kn[:, h * dh:(h + 1) * dh]) for h in range(nh)], axis=0) * scale
    r_t = lax.broadcasted_iota(I32, s_own.shape, 0) & (t_new - 1)
    c_t = lax.broadcasted_iota(I32, s_own.shape, 1)
    s_own = s_own + _t5_bias(jnp.maximum(r_t - c_t, 0), col)
    s_own = jnp.where(c_t <= r_t, s_own, NEG)

    m = jnp.maximum(jnp.max(s, axis=-1, keepdims=True), jnp.max(s_own, axis=-1, keepdims=True))
    p = jnp.exp(s - m)
    p_own = jnp.exp(s_own - m)
    l = jnp.sum(p, axis=-1, keepdims=True) + jnp.sum(p_own, axis=-1, keepdims=True)
    s_buf[...] = p
    p_own = p_own.astype(BF16)
    acc = [_dot(p_own[h * t_new:(h + 1) * t_new, :], vn[:, h * dh:(h + 1) * dh]) for h in range(nh)]

    for ci in range(n_chunks):
        sl = (n_chunks + ci) % 2
        if ci + 1 < n_chunks:
            start_v(ci + 1, 1 - sl)
        else:
            @pl.when(b + 1 < nb)
            def _():
                start_chunk(b + 1, 0, k_hbm, 1 - sl)
        wait_v(ci, sl)
        for h in range(nh):
            vh = buf[sl, h].astype(BF16)
            ph = s_buf[h * t_new:(h + 1) * t_new, ci * chunk:(ci + 1) * chunk].astype(BF16)
            acc[h] = acc[h] + _dot(ph, vh)
    for h in range(nh):
        o_ref[:, h * dh:(h + 1) * dh] = acc[h] / l[h * t_new:(h + 1) * t_new, :]


def _moba_sample_attn(page_table, q_all, k_s, v_s, rel_table, cache_k, cache_v, layer, n_p, t_new):
    n_seq, n_pages = page_table.shape
    n_layers, n_phys, page, nh, dh = cache_k.shape
    d = nh * dh
    past = n_pages * page
    cpages = min(8, n_pages // 2)
    assert n_pages % cpages == 0 and (n_pages // cpages) % 2 == 0
    rows = nh * t_new
    q_off = n_p // t_new
    trow = jnp.repeat(rel_table.T, t_new, axis=0)
    grid_spec = pltpu.PrefetchScalarGridSpec(
        num_scalar_prefetch=1,
        grid=(n_seq,),
        in_specs=[pl.BlockSpec((t_new, d), lambda b, pt: (q_off + b, 0)),
                  pl.BlockSpec((t_new, d), lambda b, pt: (b, 0)),
                  pl.BlockSpec((t_new, d), lambda b, pt: (b, 0)),
                  pl.BlockSpec((rows, REL_BUCKETS), lambda b, pt: (0, 0)),
                  pl.BlockSpec(memory_space=pl.ANY),
                  pl.BlockSpec(memory_space=pl.ANY)],
        out_specs=pl.BlockSpec((t_new, d), lambda b, pt: (b, 0)),
        scratch_shapes=[pltpu.VMEM((2, nh, cpages * page, dh), F32),
                        pltpu.VMEM((nh, past // MOBA_BLOCK, dh), F32),
                        pltpu.VMEM((rows, past), F32),
                        pltpu.SemaphoreType.DMA((2,))])
    return pl.pallas_call(
        functools.partial(_moba_sample_kernel, layer, n_pages, page, t_new, cpages),
        grid_spec=grid_spec,
        out_shape=jax.ShapeDtypeStruct((n_seq * t_new, d), F32),
        compiler_params=_cparams(("arbitrary",)),
        name="moba_sample_attn",
    )(page_table, q_all, k_s, v_s, trow, cache_k, cache_v)


def _rope_tables(pos):
    half = MLA_ROPE // 2
    inv_freq = ROPE_THETA ** (-jnp.arange(half, dtype=F32) / half)
    ang = pos.astype(F32)[:, None] * inv_freq
    z = jnp.zeros((pos.shape[0], LANES - MLA_ROPE), F32)
    cos = jnp.concatenate([jnp.cos(ang), jnp.cos(ang), z], axis=-1)
    sin = jnp.concatenate([jnp.sin(ang), jnp.sin(ang), z], axis=-1)
    return cos, sin


def kernel(x_prompt, x_sample, cache_mla_ckv, cache_mla_kpe, cache_moba_k, cache_moba_v, page_table, c_prompt, c_sample, w_ada, b_ada, g_norm_mix, g_norm_ffn, g_final, w_dq, g_q, w_uq, w_dkv, g_kv, w_uk, w_uv, w_o_mla, w_qkv_moba, w_o_moba, rel_bias_table, w_router_group, w_router_expert, w_exp_gate, w_exp_up, w_exp_down):
    n_batch, seq_len, d = x_prompt.shape
    n_seq, t_new, _ = x_sample.shape
    depth = w_ada.shape[0]
    page = cache_mla_ckv.shape[2]
    past = page_table.shape[1] * page
    n_p, n_s = n_batch * seq_len, n_seq * t_new
    tile = min(TOK_TILE, n_s)
    assert n_s % tile == 0 and seq_len % tile == 0 and tile % t_new == 0

    xp = x_prompt.reshape(n_p, d)
    xs = x_sample.reshape(n_s, d)
    m = _ada(jnp.concatenate([c_sample, c_prompt], axis=0), w_ada, b_ada)
    mp4 = m[:, n_seq:, :].reshape(depth, n_batch, 1, 6 * d)
    ms3 = jnp.repeat(m[:, :n_seq, :], t_new, axis=1)

    cos_p, sin_p = _rope_tables(jnp.arange(seq_len))
    cos_s, sin_s = _rope_tables(past + (jnp.arange(tile) % t_new))
    cos_all = jnp.concatenate([cos_p, cos_s], axis=0)
    sin_all = jnp.concatenate([sin_p, sin_s], axis=0)

    outs = {}
    x = (xp, xs)
    for i in range(depth):
        if i % 2 == 0:
            la = i // 2
            if not isinstance(x, tuple):
                x = (x[:n_p], x[n_p:])
            ckv_p, ckv_s, kpe_p, kpe_s, q_all, k_all, q_s, k_s = _mla_proj(
                x[0], x[1], mp4, ms3, i, g_norm_mix[i], cos_all, sin_all,
                w_dq[la], g_q[la], w_uq[la], w_dkv[la], g_kv[la], w_uk[la], seq_len, tile)
            o_p = _mla_prompt_attn(q_all, k_all, n_batch, seq_len, min(ATT_TILE, seq_len))
            o_s = _mla_sample_attn(page_table, q_s, k_s, cache_mla_ckv, cache_mla_kpe, la, t_new)
            outs.setdefault("ckv_p", []).append(ckv_p.reshape(n_batch, seq_len, MLA_KV_RANK))
            outs.setdefault("kpe_p", []).append(kpe_p.reshape(n_batch, seq_len, MLA_ROPE))
            outs.setdefault("ckv_s", []).append(ckv_s.reshape(n_seq, t_new, MLA_KV_RANK))
            outs.setdefault("kpe_s", []).append(kpe_s.reshape(n_seq, t_new, MLA_ROPE))
            x_new, h2, route = _attn_out(o_p, o_s, x, mp4, ms3, i, g_norm_ffn[i], w_uv[la], w_o_mla[la],
                                         w_router_group[i], w_router_expert[i], seq_len, tile)
        else:
            lb = i // 2
            xu = x if not isinstance(x, tuple) else jnp.concatenate(x, axis=0)
            q_all, k_p, k_s, v_p, v_s, k_b, v_b, kmean = _moba_qkv(
                xu, mp4, ms3, i, g_norm_mix[i], w_qkv_moba[lb], n_p, seq_len, tile)
            o_p = _moba_prompt_attn(q_all, k_b, v_b, kmean, rel_bias_table, n_batch, seq_len)
            o_s = _moba_sample_attn(page_table, q_all, k_s, v_s, rel_bias_table, cache_moba_k, cache_moba_v,
                                    lb, n_p, t_new)
            hd = (MOBA_HEADS, MOBA_HEAD_DIM)
            outs.setdefault("k_p", []).append(k_p.reshape(n_batch, seq_len, *hd))
            outs.setdefault("v_p", []).append(v_p.reshape(n_batch, seq_len, *hd))
            outs.setdefault("k_s", []).append(k_s.reshape(n_seq, t_new, *hd))
            outs.setdefault("v_s", []).append(v_s.reshape(n_seq, t_new, *hd))
            x_new, h2, route = _attn_out(o_p, o_s, xu, mp4, ms3, i, g_norm_ffn[i], None, w_o_moba[lb],
                                         w_router_group[i], w_router_expert[i], seq_len, tile)
        y2 = _moe(h2, route, w_exp_gate, w_exp_up, w_exp_down, i, MOE_TILE)
        last = i == depth - 1
        x = _combine(x_new, y2, mp4, ms3, i, g_final if last else None, n_p, seq_len, tile)
    y_p, y_s = x
    st = lambda key: jnp.stack(outs[key])
    return (y_p.reshape(n_batch, seq_len, d), y_s.reshape(n_seq, t_new, d),
            st("ckv_p"), st("kpe_p"), st("k_p"), st("v_p"), st("ckv_s"), st("kpe_s"), st("k_s"), st("v_s"))
```

```python
import functools
import math

import numpy as np
import jax
import jax.numpy as jnp
from jax import lax
from jax.experimental import pallas as pl
from jax.experimental.pallas import tpu as pltpu

F32 = jnp.float32
BF16 = jnp.bfloat16
I32 = jnp.int32

EPS = 1e-6
ROPE_THETA = 10000.0
MLA_HEADS = 8
MLA_NOPE = 128
MLA_ROPE = 64
MLA_KV_RANK = 256
MLA_V = 128
MLA_SCALE = (MLA_NOPE + MLA_ROPE) ** -0.5
MLA_QK = MLA_KV_RANK + 128
MOBA_HEADS = 8
MOBA_HEAD_DIM = 128
MOBA_BLOCK = 256
MOBA_TOPK = 3
REL_BUCKETS = 32
REL_MAX_DIST = 128
N_GROUPS = 4
EXPERTS_PER_GROUP = 8
N_EXPERTS = N_GROUPS * EXPERTS_PER_GROUP
LANES = 128
NEG = -1e30

TOK_TILE = 512
ATT_TILE = 256
MOE_TILE = 256
VMEM_LIMIT = 56 * 1024 * 1024


def _t5_thresholds():
    max_exact = REL_BUCKETS // 2
    d = np.arange(1, 4 * REL_MAX_DIST).astype(np.float32)
    t = (np.log(d / np.float32(max_exact)) / np.float32(math.log(REL_MAX_DIST / max_exact))
         * np.float32(REL_BUCKETS - max_exact)).astype(np.float32)
    b = np.where(d < max_exact, d.astype(np.int32), np.minimum(max_exact + t.astype(np.int32), REL_BUCKETS - 1))
    return [int(d[np.argmax(b >= k)]) for k in range(REL_BUCKETS)]


T5_THR = _t5_thresholds()


def _cparams(sem):
    return pltpu.CompilerParams(dimension_semantics=sem, vmem_limit_bytes=VMEM_LIMIT)


def _rms(x, g):
    return x * lax.rsqrt(jnp.mean(x * x, axis=-1, keepdims=True) + EPS) * g


def _sigmoid(x):
    return 1.0 / (1.0 + jnp.exp(-x))


def _dot(a, b):
    return jnp.dot(a, b, preferred_element_type=F32)


def _dot_nt(a, b):
    return lax.dot_general(a, b, (((1,), (1,)), ((), ())), preferred_element_type=F32)


def _dot_f32(a, b):
    return jnp.dot(a, b, preferred_element_type=F32, precision=lax.Precision.HIGHEST)


def _dot_nt_f32(a, b):
    return lax.dot_general(a, b, (((1,), (1,)), ((), ())), preferred_element_type=F32,
                           precision=lax.Precision.HIGHEST)


def _split_specs(n_pt, cols, tile):
    return [pl.BlockSpec((tile, cols), lambda i: (jnp.minimum(i, n_pt - 1), 0)),
            pl.BlockSpec((tile, cols), lambda i: (jnp.maximum(i - n_pt, 0), 0))]


def _mod_specs(layer, k, n_pt, tiles_per_seq, n_batch, d, tile):
    return [pl.BlockSpec((None, None, 1, d),
                         lambda i: (layer, jnp.minimum(i // tiles_per_seq, n_batch - 1), 0, k)),
            pl.BlockSpec((None, tile, d), lambda i: (layer, jnp.maximum(i - n_pt, 0), k))]


def _full(shape):
    nd = len(shape)
    return pl.BlockSpec(shape, lambda *_: (0,) * nd)


def _ada_kernel(c_ref, w_ref, b_ref, o_ref):
    c = c_ref[...]
    s = c * _sigmoid(c)
    o_ref[...] = _dot(s.astype(BF16), w_ref[...].astype(BF16)) + b_ref[...]


def _ada(c_all, w_ada, b_ada):
    depth, d, d6 = w_ada.shape
    n = c_all.shape[0]
    cols = 1536
    return pl.pallas_call(
        _ada_kernel,
        grid=(depth, d6 // cols),
        in_specs=[pl.BlockSpec((n, d), lambda l, j: (0, 0)),
                  pl.BlockSpec((None, d, cols), lambda l, j: (l, 0, j)),
                  pl.BlockSpec((None, 1, cols), lambda l, j: (l, 0, j))],
        out_specs=pl.BlockSpec((None, n, cols), lambda l, j: (l, 0, j)),
        out_shape=jax.ShapeDtypeStruct((depth, n, d6), F32),
        compiler_params=_cparams(("arbitrary", "arbitrary")),
        name="ada_params",
    )(c_all, w_ada, b_ada.reshape(depth, 1, d6))


def _mla_proj_kernel(n_pt, xp_ref, xs_ref, shp_ref, shs_ref, scp_ref, scs_ref, gmix_ref, cos_ref, sin_ref,
                     wdq_ref, gq_ref, wuqn_ref, wuqp_ref, wuqr_ref, wuk_ref, wdkv_ref, wdkvr_ref, gkv_ref,
                     ckvp_ref, ckvs_ref, kpep_ref, kpes_ref, q_ref, k_ref, qs_ref, ks_ref):
    i = pl.program_id(0)
    is_s = i >= n_pt
    x = jnp.where(is_s, xs_ref[...], xp_ref[...])
    shift = jnp.where(is_s, shs_ref[...], shp_ref[...])
    scale = jnp.where(is_s, scs_ref[...], scp_ref[...])
    h = (_rms(x, gmix_ref[...]) * (1.0 + scale) + shift).astype(BF16)
    cos = cos_ref[...]
    sin = sin_ref[...]

    cq = _dot(h, wdq_ref[...])
    cqn = _rms(cq, gq_ref[...]).astype(BF16)
    qn = _dot(cqn, wuqn_ref[...]).astype(BF16)
    qpe = _dot(cqn, wuqp_ref[...])
    qrot = _dot(cqn, wuqr_ref[...])
    kv = _dot(h, wdkv_ref[...])
    kvrot = _dot(h, wdkvr_ref[...])
    ckv = _rms(kv[:, :MLA_KV_RANK], gkv_ref[...])
    kpe = kv[:, MLA_KV_RANK:] * cos + kvrot * sin

    pieces = []
    for hh in range(MLA_HEADS):
        sl = slice(hh * LANES, (hh + 1) * LANES)
        pieces.append(_dot(qn[:, sl], wuk_ref[hh]))
        pieces.append(qpe[:, sl] * cos + qrot[:, sl] * sin)
    q = jnp.concatenate(pieces, axis=-1)
    k = jnp.concatenate([ckv, kpe], axis=-1)
    q_ref[...] = q.astype(BF16)
    k_ref[...] = k.astype(BF16)

    @pl.when(jnp.logical_not(is_s))
    def _():
        ckvp_ref[...] = ckv
        kpep_ref[...] = kpe[:, :MLA_ROPE]

    @pl.when(is_s)
    def _():
        ckvs_ref[...] = ckv
        kpes_ref[...] = kpe[:, :MLA_ROPE]
        qs_ref[...] = q
        ks_ref[...] = k


def _rot_cols(w):
    half = MLA_ROPE // 2
    return jnp.concatenate([-w[..., half:], w[..., :half]], axis=-1)


def _mla_proj(xp, xs, mp4, ms3, layer, g_mix, cos_all, sin_all, w_dq, g_q, w_uq, w_dkv, g_kv, w_uk,
              seq_len, tile):
    n_p, d = xp.shape
    n_s = xs.shape[0]
    n_pt, n_st = n_p // tile, n_s // tile
    tps = seq_len // tile
    n_batch = n_p // seq_len
    qr = w_dq.shape[1]
    hq = MLA_HEADS * LANES
    wq = w_uq.reshape(qr, MLA_HEADS, MLA_NOPE + MLA_ROPE)
    w_n = wq[:, :, :MLA_NOPE].reshape(qr, hq).astype(BF16)
    zpad = jnp.zeros((qr, MLA_HEADS, LANES - MLA_ROPE), F32)
    w_p = jnp.concatenate([wq[:, :, MLA_NOPE:], zpad], axis=-1).reshape(qr, hq).astype(BF16)
    w_r = jnp.concatenate([_rot_cols(wq[:, :, MLA_NOPE:]), zpad], axis=-1).reshape(qr, hq).astype(BF16)
    w_ukt = jnp.transpose(w_uk, (1, 2, 0)).astype(BF16)
    zk = jnp.zeros((d, LANES - MLA_ROPE), F32)
    w_kv = jnp.concatenate([w_dkv, zk], axis=-1).astype(BF16)
    w_kvr = jnp.concatenate([_rot_cols(w_dkv[:, MLA_KV_RANK:]), zk], axis=-1).astype(BF16)
    n_tab = cos_all.shape[0] // tile - 1

    tab_spec = pl.BlockSpec((tile, LANES), lambda i: (jnp.where(i < n_pt, i % tps, n_tab), 0))
    in_specs = (_split_specs(n_pt, d, tile)
                + _mod_specs(layer, 0, n_pt, tps, n_batch, d, tile)
                + _mod_specs(layer, 1, n_pt, tps, n_batch, d, tile)
                + [_full((1, d)), tab_spec, tab_spec,
                   _full((d, qr)), _full((1, qr)), _full((qr, hq)), _full((qr, hq)), _full((qr, hq)),
                   _full((MLA_HEADS, MLA_NOPE, MLA_KV_RANK)), _full((d, MLA_QK)), _full((d, LANES)),
                   _full((1, MLA_KV_RANK))])
    n_all = n_p + n_s
    out_shape = (jax.ShapeDtypeStruct((n_p, MLA_KV_RANK), F32), jax.ShapeDtypeStruct((n_s, MLA_KV_RANK), F32),
                 jax.ShapeDtypeStruct((n_p, MLA_ROPE), F32), jax.ShapeDtypeStruct((n_s, MLA_ROPE), F32),
                 jax.ShapeDtypeStruct((n_all, MLA_HEADS * MLA_QK), BF16),
                 jax.ShapeDtypeStruct((n_all, MLA_QK), BF16),
                 jax.ShapeDtypeStruct((n_s, MLA_HEADS * MLA_QK), F32),
                 jax.ShapeDtypeStruct((n_s, MLA_QK), F32))
    out_specs = (_split_specs(n_pt, MLA_KV_RANK, tile) + _split_specs(n_pt, MLA_ROPE, tile)
                 + [pl.BlockSpec((tile, MLA_HEADS * MLA_QK), lambda i: (i, 0)),
                    pl.BlockSpec((tile, MLA_QK), lambda i: (i, 0)),
                    _split_specs(n_pt, MLA_HEADS * MLA_QK, tile)[1],
                    _split_specs(n_pt, MLA_QK, tile)[1]])
    return pl.pallas_call(
        functools.partial(_mla_proj_kernel, n_pt),
        grid=(n_pt + n_st,),
        in_specs=in_specs, out_specs=out_specs, out_shape=out_shape,
        compiler_params=_cparams(("arbitrary",)),
        name="mla_proj",
    )(xp, xs, mp4, ms3, mp4, ms3, g_mix.reshape(1, d), cos_all, sin_all,
      w_dq.astype(BF16), g_q.reshape(1, qr), w_n, w_p, w_r, w_ukt, w_kv, w_kvr, g_kv.reshape(1, MLA_KV_RANK))


def _mla_prompt_kernel(tq, q_ref, k_ref, o_ref, acc_ref):
    i = pl.program_id(1)
    m_rows = MLA_HEADS * tq
    q = q_ref[...]
    qs = jnp.concatenate([q[:, h * MLA_QK:(h + 1) * MLA_QK] for h in range(MLA_HEADS)], axis=0)

    def step(kblk, m, l, masked):
        s = _dot_nt(qs, kblk) * MLA_SCALE
        if masked:
            r = lax.broadcasted_iota(I32, s.shape, 0) & (tq - 1)
            c = lax.broadcasted_iota(I32, s.shape, 1)
            s = jnp.where(c <= r, s, NEG)
        m_new = jnp.maximum(m, jnp.max(s, axis=-1, keepdims=True))
        alpha = jnp.exp(m - m_new)
        p = jnp.exp(s - m_new)
        l_new = alpha * l + jnp.sum(p, axis=-1, keepdims=True)
        pv = _dot(p.astype(BF16), kblk[:, :MLA_KV_RANK])
        return m_new, l_new, alpha, pv

    kd = k_ref[pl.ds(pl.multiple_of(i * tq, tq), tq), :]
    m0 = jnp.full((m_rows, 1), NEG, F32)
    l0 = jnp.zeros((m_rows, 1), F32)
    m, l, _, pv = step(kd, m0, l0, True)
    acc_ref[...] = pv

    def body(j, carry):
        m, l = carry
        kb = k_ref[pl.ds(pl.multiple_of(j * tq, tq), tq), :]
        m, l, alpha, pv = step(kb, m, l, False)
        acc_ref[...] = acc_ref[...] * alpha + pv
        return m, l

    m, l = lax.fori_loop(0, i, body, (m, l))
    o = acc_ref[...] / l
    for h in range(MLA_HEADS):
        o_ref[:, h * MLA_KV_RANK:(h + 1) * MLA_KV_RANK] = o[h * tq:(h + 1) * tq, :].astype(o_ref.dtype)


def _mla_prompt_attn(q_all, k_all, n_batch, seq_len, tq):
    nq = seq_len // tq
    return pl.pallas_call(
        functools.partial(_mla_prompt_kernel, tq),
        grid=(n_batch, nq),
        in_specs=[pl.BlockSpec((tq, MLA_HEADS * MLA_QK), lambda b, i: (b * nq + i, 0)),
                  pl.BlockSpec((seq_len, MLA_QK), lambda b, i: (b, 0))],
        out_specs=pl.BlockSpec((tq, MLA_HEADS * MLA_KV_RANK), lambda b, i: (b * nq + i, 0)),
        out_shape=jax.ShapeDtypeStruct((n_batch * seq_len, MLA_HEADS * MLA_KV_RANK), BF16),
        scratch_shapes=[pltpu.VMEM((MLA_HEADS * tq, MLA_KV_RANK), F32)],
        compiler_params=_cparams(("arbitrary", "arbitrary")),
        name="mla_prompt_attn",
    )(q_all, k_all)


def _mla_sample_kernel(layer, n_pages, page, t_new, chunk,
                       pt_ref, q_ref, k_ref, ckv_hbm, kpe_hbm, o_ref,
                       ckv_buf, kpe_buf, kb_buf, s_buf, sem):
    b = pl.program_id(0)
    nb = pl.num_programs(0)
    slot = b % 2
    past = n_pages * page

    def copies(seq, sl):
        out = []
        for j in range(n_pages):
            pid = pt_ref[seq, j]
            out.append(pltpu.make_async_copy(ckv_hbm.at[layer, pid], ckv_buf.at[sl, pl.ds(j * page, page)], sem.at[0, sl]))
            out.append(pltpu.make_async_copy(kpe_hbm.at[layer, pid], kpe_buf.at[sl, j], sem.at[1, sl]))
        return out

    @pl.when(b == 0)
    def _():
        for cp in copies(0, 0):
            cp.start()

    @pl.when(b + 1 < nb)
    def _():
        for cp in copies(b + 1, 1 - slot):
            cp.start()

    for cp in copies(b, slot):
        cp.wait()

    q = q_ref[...]
    qs = jnp.concatenate([q[:, h * MLA_QK:(h + 1) * MLA_QK] for h in range(MLA_HEADS)], axis=0).astype(BF16)
    ql = qs[:, :MLA_KV_RANK]
    qp = qs[:, MLA_KV_RANK:MLA_KV_RANK + MLA_ROPE]
    ppc = chunk // page
    for c in range(past // chunk):
        kc = ckv_buf[slot, c * chunk:(c + 1) * chunk, :].astype(BF16)
        kb_buf[c * chunk:(c + 1) * chunk, :] = kc
        kpe_c = jnp.concatenate([kpe_buf[slot, c * ppc + j] for j in range(ppc)], axis=-1).astype(BF16)
        s_pe = _dot(qp, kpe_c)
        s_buf[:, c * chunk:(c + 1) * chunk] = (_dot_nt(ql, kc) + s_pe) * MLA_SCALE

    kn = k_ref[...].astype(BF16)
    s_new = _dot_nt(qs, kn) * MLA_SCALE
    r = lax.broadcasted_iota(I32, s_new.shape, 0) & (t_new - 1)
    c_ = lax.broadcasted_iota(I32, s_new.shape, 1)
    s_new = jnp.where(c_ <= r, s_new, NEG)
    s = s_buf[...]
    m = jnp.maximum(jnp.max(s, axis=-1, keepdims=True), jnp.max(s_new, axis=-1, keepdims=True))
    p_new = jnp.exp(s_new - m)
    l = jnp.sum(p_new, axis=-1, keepdims=True)
    acc = _dot(p_new.astype(BF16), kn[:, :MLA_KV_RANK])
    for c in range(past // chunk):
        p = jnp.exp(s_buf[:, c * chunk:(c + 1) * chunk] - m)
        l = l + jnp.sum(p, axis=-1, keepdims=True)
        acc = acc + _dot(p.astype(BF16), kb_buf[c * chunk:(c + 1) * chunk, :])
    o = acc / l
    for h in range(MLA_HEADS):
        o_ref[:, h * MLA_KV_RANK:(h + 1) * MLA_KV_RANK] = o[h * t_new:(h + 1) * t_new, :]


def _mla_sample_attn(page_table, q_s, k_s, cache_ckv, cache_kpe, layer, t_new):
    n_seq, n_pages = page_table.shape
    page = cache_ckv.shape[2]
    past = n_pages * page
    chunk = min(1024, past)
    rows = MLA_HEADS * t_new
    cache_kpe = jnp.swapaxes(cache_kpe, 2, 3)
    grid_spec = pltpu.PrefetchScalarGridSpec(
        num_scalar_prefetch=1,
        grid=(n_seq,),
        in_specs=[pl.BlockSpec((t_new, MLA_HEADS * MLA_QK), lambda b, pt: (b, 0)),
                  pl.BlockSpec((t_new, MLA_QK), lambda b, pt: (b, 0)),
                  pl.BlockSpec(memory_space=pl.ANY),
                  pl.BlockSpec(memory_space=pl.ANY)],
        out_specs=pl.BlockSpec((t_new, MLA_HEADS * MLA_KV_RANK), lambda b, pt: (b, 0)),
        scratch_shapes=[pltpu.VMEM((2, past, MLA_KV_RANK), F32),
                        pltpu.VMEM((2, n_pages, MLA_ROPE, page), F32),
                        pltpu.VMEM((past, MLA_KV_RANK), BF16),
                        pltpu.VMEM((rows, past), F32),
                        pltpu.SemaphoreType.DMA((2, 2))])
    return pl.pallas_call(
        functools.partial(_mla_sample_kernel, layer, n_pages, page, t_new, chunk),
        grid_spec=grid_spec,
        out_shape=jax.ShapeDtypeStruct((n_seq * t_new, MLA_HEADS * MLA_KV_RANK), F32),
        compiler_params=_cparams(("arbitrary",)),
        name="mla_sample_attn",
    )(page_table, q_s, k_s, cache_ckv, cache_kpe)


def _route(z):
    lane = lax.broadcasted_iota(I32, z.shape, 1)
    big = jnp.int32(1 << 20)
    gmask = lane < N_GROUPS
    zg = jnp.where(gmask, z, NEG)
    gmax = jnp.max(zg, axis=-1, keepdims=True)
    gidx = jnp.min(jnp.where(jnp.logical_and(gmask, zg == gmax), lane, big), axis=-1, keepdims=True)
    gsum = jnp.sum(jnp.where(gmask, jnp.exp(zg - gmax), 0.0), axis=-1, keepdims=True)
    g_top = 1.0 / gsum
    lo = N_GROUPS + gidx * EXPERTS_PER_GROUP
    emask = jnp.logical_and(lane >= lo, lane < lo + EXPERTS_PER_GROUP)
    z1 = jnp.where(emask, z, NEG)
    e1 = jnp.max(z1, axis=-1, keepdims=True)
    i1 = jnp.min(jnp.where(jnp.logical_and(emask, z1 == e1), lane, big), axis=-1, keepdims=True)
    z2 = jnp.where(lane == i1, NEG, z1)
    e2 = jnp.max(z2, axis=-1, keepdims=True)
    i2 = jnp.min(jnp.where(jnp.logical_and(emask, jnp.logical_and(z2 == e2, lane != i1)), lane, big),
                 axis=-1, keepdims=True)
    t = jnp.exp(e2 - e1)
    w1 = g_top / (1.0 + t)
    w2 = g_top * t / (1.0 + t)
    out = jnp.where(lane == 0, (i1 - N_GROUPS).astype(F32),
                    jnp.where(lane == 1, (i2 - N_GROUPS).astype(F32),
                              jnp.where(lane == 2, w1, jnp.where(lane == 3, w2, 0.0))))
    return out


def _attn_out_kernel(n_pt, split_x, use_uv, *refs):
    refs = list(refs)
    op_ref, os_ref = refs[:2]
    refs = refs[2:]
    if split_x:
        xp_ref, xs_ref = refs[:2]
        refs = refs[2:]
    else:
        x_ref = refs[0]
        refs = refs[1:]
    (gp_ref, gs_ref, shp_ref, shs_ref, scp_ref, scs_ref, gffn_ref) = refs[:7]
    refs = refs[7:]
    if use_uv:
        wuv_ref = refs[0]
        refs = refs[1:]
    wo_ref, wrh_ref, wrl_ref, xo_ref, h2_ref, rt_ref = refs

    i = pl.program_id(0)
    is_s = i >= n_pt
    o = jnp.where(is_s, os_ref[...].astype(BF16), op_ref[...])
    if split_x:
        x = jnp.where(is_s, xs_ref[...], xp_ref[...])
    else:
        x = x_ref[...]
    gate = jnp.where(is_s, gs_ref[...], gp_ref[...])
    shift = jnp.where(is_s, shs_ref[...], shp_ref[...])
    scale = jnp.where(is_s, scs_ref[...], scp_ref[...])
    if use_uv:
        o = jnp.concatenate(
            [_dot(o[:, h * MLA_KV_RANK:(h + 1) * MLA_KV_RANK], wuv_ref[h]) for h in range(MLA_HEADS)],
            axis=-1).astype(BF16)
    a = _dot(o, wo_ref[...])
    xn = x + gate * a
    xo_ref[...] = xn
    h2 = _rms(xn, gffn_ref[...]) * (1.0 + scale) + shift
    h2_ref[...] = h2
    h2h = h2.astype(BF16)
    h2l = (h2 - h2h.astype(F32)).astype(BF16)
    z = _dot(h2h, wrh_ref[...]) + (_dot(h2h, wrl_ref[...]) + _dot(h2l, wrh_ref[...]))
    rt_ref[...] = _route(z)


def _attn_out(o_p, o_s, x, mp4, ms3, layer, g_ffn, w_uv, w_o, w_rg, w_re, seq_len, tile):
    split_x = isinstance(x, tuple)
    n_p = o_p.shape[0]
    n_s = o_s.shape[0]
    din = o_p.shape[1]
    d = w_o.shape[1]
    n_pt, n_st = n_p // tile, n_s // tile
    tps = seq_len // tile
    n_batch = n_p // seq_len
    use_uv = w_uv is not None
    wr = jnp.concatenate([w_rg, w_re, jnp.zeros((d, LANES - N_GROUPS - N_EXPERTS), F32)], axis=-1)
    in_specs = _split_specs(n_pt, din, tile)
    args = [o_p, o_s]
    if split_x:
        in_specs += _split_specs(n_pt, d, tile)
        args += list(x)
    else:
        in_specs += [pl.BlockSpec((tile, d), lambda i: (i, 0))]
        args += [x]
    for k in (2, 3, 4):
        in_specs += _mod_specs(layer, k, n_pt, tps, n_batch, d, tile)
        args += [mp4, ms3]
    in_specs += [_full((1, d))]
    args += [g_ffn.reshape(1, d)]
    if use_uv:
        in_specs += [_full((MLA_HEADS, MLA_KV_RANK, MLA_V))]
        args += [jnp.transpose(w_uv, (1, 0, 2)).astype(BF16)]
    wr_hi = wr.astype(BF16)
    wr_lo = (wr - wr_hi.astype(F32)).astype(BF16)
    in_specs += [_full(w_o.shape), _full((d, LANES)), _full((d, LANES))]
    args += [w_o.astype(BF16), wr_hi, wr_lo]
    n_all = n_p + n_s
    tok = lambda cols: pl.BlockSpec((tile, cols), lambda i: (i, 0))
    return pl.pallas_call(
        functools.partial(_attn_out_kernel, n_pt, split_x, use_uv),
        grid=(n_pt + n_st,),
        in_specs=in_specs,
        out_specs=[tok(d), tok(d), tok(LANES)],
        out_shape=(jax.ShapeDtypeStruct((n_all, d), F32), jax.ShapeDtypeStruct((n_all, d), F32),
                   jax.ShapeDtypeStruct((n_all, LANES), F32)),
        compiler_params=_cparams(("arbitrary",)),
        name="attn_out_router",
    )(*args)


def _moe_kernel(tm, te_ref, tr_ref, src_ref, srcn_ref, dst_ref, w_ref, h_hbm, wg_ref, wu_ref, wd_ref,
                y_hbm, xbuf, ybuf, gsem, ssem):
    t = pl.program_id(0)
    nt = pl.num_programs(0)
    slot = t % 2
    rows = tr_ref[t]
    rows_next = tr_ref[jnp.minimum(t + 1, nt - 1)]
    rows_prev = tr_ref[jnp.maximum(t - 1, 0)]

    def gather_copy(idx_ref, r, sl):
        return pltpu.make_async_copy(h_hbm.at[pl.ds(idx_ref[0, 0, r], 1)], xbuf.at[sl, pl.ds(r, 1)], gsem.at[sl])

    def scatter_copy(r):
        return pltpu.make_async_copy(ybuf.at[pl.ds(r, 1)], y_hbm.at[pl.ds(dst_ref[0, 0, r], 1)], ssem.at[0])

    @pl.when(jnp.logical_and(t == 0, rows > 0))
    def _():
        for r in range(tm):
            gather_copy(src_ref, r, 0).start()

    @pl.when(jnp.logical_and(t + 1 < nt, rows_next > 0))
    def _():
        for r in range(tm):
            gather_copy(srcn_ref, r, 1 - slot).start()

    @pl.when(jnp.logical_and(t > 0, rows_prev > 0))
    def _():
        for r in range(tm):
            scatter_copy(r).wait()

    @pl.when(rows > 0)
    def _():
        for r in range(tm):
            gather_copy(src_ref, r, slot).wait()
        xb = xbuf[slot].astype(BF16)
        a = _dot(xb, wg_ref[...].astype(BF16))
        u = _dot(xb, wu_ref[...].astype(BF16))
        act = (a * _sigmoid(a)) * u * w_ref[...]
        ybuf[...] = _dot(act.astype(BF16), wd_ref[...].astype(BF16))
        for r in range(tm):
            scatter_copy(r).start()

    @pl.when(t == nt - 1)
    def _():
        @pl.when(rows > 0)
        def _():
            for r in range(tm):
                scatter_copy(r).wait()
        ybuf[...] = jnp.zeros(ybuf.shape, ybuf.dtype)
        fill = pltpu.make_async_copy(ybuf, y_hbm.at[pl.ds(y_hbm.shape[0] - tm, tm)], ssem.at[0])
        fill.start()
        fill.wait()


def _moe(h2, route, w_gate, w_up, w_down, layer, tm):
    n, d = h2.shape
    _, n_exp, _, f = w_gate.shape
    eid = route[:, :2].astype(I32)
    ew = route[:, 2:4]
    n2 = 2 * n
    flat = eid.reshape(n2)
    order = jnp.argsort(flat, stable=True).astype(I32)
    counts = jnp.sum((flat[:, None] == jnp.arange(n_exp, dtype=I32)[None, :]).astype(I32), axis=0)
    padded = ((counts + tm - 1) // tm) * tm
    pend = jnp.cumsum(padded)
    pstart = pend - padded
    ustart = jnp.cumsum(counts) - counts
    n_tiles = n2 // tm + n_exp
    tstart = jnp.arange(n_tiles, dtype=I32) * tm
    te = jnp.minimum(jnp.sum((pend[None, :] <= tstart[:, None]).astype(I32), axis=1), n_exp - 1)
    r0 = tstart - pstart[te]
    tr = jnp.clip(counts[te] - r0, 0, tm).astype(I32)
    lane = jnp.arange(tm, dtype=I32)[None, :]
    pos = (ustart[te] + r0)[:, None] + lane
    fidx = order[jnp.clip(pos, 0, n2 - 1)]
    tok = fidx >> 1
    src = tok.reshape(n_tiles, 1, tm)
    dst = jnp.where(lane < tr[:, None], (fidx & 1) * n + tok, n2 + lane).reshape(n_tiles, 1, tm)
    wslot = ew.reshape(n2)[fidx].reshape(n_tiles, tm, 1)

    smem_blk = lambda f_: pl.BlockSpec((1, 1, tm), f_, memory_space=pltpu.SMEM)
    wspec = lambda a_, b_: pl.BlockSpec((None, None, a_, b_), lambda t, te_, tr_: (layer, te_[t], 0, 0))
    grid_spec = pltpu.PrefetchScalarGridSpec(
        num_scalar_prefetch=2,
        grid=(n_tiles,),
        in_specs=[smem_blk(lambda t, te_, tr_: (t, 0, 0)),
                  smem_blk(lambda t, te_, tr_: (jnp.minimum(t + 1, n_tiles - 1), 0, 0)),
                  smem_blk(lambda t, te_, tr_: (t, 0, 0)),
                  pl.BlockSpec((None, tm, 1), lambda t, te_, tr_: (t, 0, 0)),
                  pl.BlockSpec(memory_space=pl.ANY),
                  wspec(d, f), wspec(d, f), wspec(f, d)],
        out_specs=pl.BlockSpec(memory_space=pl.ANY),
        scratch_shapes=[pltpu.VMEM((2, tm, d), F32), pltpu.VMEM((tm, d), F32),
                        pltpu.SemaphoreType.DMA((2,)), pltpu.SemaphoreType.DMA((1,))])
    return pl.pallas_call(
        functools.partial(_moe_kernel, tm),
        grid_spec=grid_spec,
        out_shape=jax.ShapeDtypeStruct((n2 + tm, d), F32),
        compiler_params=_cparams(("arbitrary",)),
        name="moe_experts",
    )(te, tr, src, src, dst, wslot, h2, w_gate, w_up, w_down)


def _combine_kernel(n_pt, final, d, *refs):
    if final:
        x_ref, y0_ref, y1_ref, gp_ref, gs_ref, gf_ref, op_ref, os_ref = refs
    else:
        x_ref, y0_ref, y1_ref, gp_ref, gs_ref, o_ref = refs
    i = pl.program_id(0)
    is_s = i >= n_pt
    gate = jnp.where(is_s, gs_ref[...], gp_ref[...])
    xn = x_ref[...] + gate * (y0_ref[...] + y1_ref[...])
    if final:
        out = _rms(xn, gf_ref[...])

        @pl.when(jnp.logical_not(is_s))
        def _():
            op_ref[...] = out

        @pl.when(is_s)
        def _():
            os_ref[...] = out
    else:
        o_ref[...] = xn


def _combine(x, y2, mp4, ms3, layer, g_final, n_p, seq_len, tile):
    n_all, d = x.shape
    n_s = n_all - n_p
    n_pt, n_st = n_p // tile, n_s // tile
    tps = seq_len // tile
    n_batch = n_p // seq_len
    final = g_final is not None
    tok = lambda cols: pl.BlockSpec((tile, cols), lambda i: (i, 0))
    n_tt = n_all // tile
    in_specs = ([tok(d), tok(d), pl.BlockSpec((tile, d), lambda i: (n_tt + i, 0))]
                + _mod_specs(layer, 5, n_pt, tps, n_batch, d, tile))
    args = [x, y2, y2, mp4, ms3]
    if final:
        in_specs += [_full((1, d))]
        args += [g_final.reshape(1, d)]
        out_specs = _split_specs(n_pt, d, tile)
        out_shape = (jax.ShapeDtypeStruct((n_p, d), F32), jax.ShapeDtypeStruct((n_s, d), F32))
    else:
        out_specs = tok(d)
        out_shape = jax.ShapeDtypeStruct((n_all, d), F32)
    return pl.pallas_call(
        functools.partial(_combine_kernel, n_pt, final, d),
        grid=(n_pt + n_st,),
        in_specs=in_specs, out_specs=out_specs, out_shape=out_shape,
        compiler_params=_cparams(("arbitrary",)),
        name="moe_combine",
    )(*args)


def _moba_qkv_kernel(n_pt, d, x_ref, shp_ref, shs_ref, scp_ref, scs_ref, gmix_ref, w_ref,
                     q_ref, kp_ref, ks_ref, vp_ref, vs_ref, kb_ref, vb_ref, ksum_ref):
    i = pl.program_id(0)
    is_s = i >= n_pt
    shift = jnp.where(is_s, shs_ref[...], shp_ref[...])
    scale = jnp.where(is_s, scs_ref[...], scp_ref[...])
    h = (_rms(x_ref[...], gmix_ref[...]) * (1.0 + scale) + shift).astype(BF16)
    qkv = _dot(h, w_ref[...])
    q_ref[...] = qkv[:, :d]

    @pl.when(jnp.logical_not(is_s))
    def _():
        k = qkv[:, d:2 * d]
        v = qkv[:, 2 * d:]
        kp_ref[...] = k
        vp_ref[...] = v
        vb_ref[...] = v.astype(BF16)
        nb = k.shape[0] // MOBA_BLOCK
        for n in range(nb):
            kb_ref[n] = jnp.transpose(k[n * MOBA_BLOCK:(n + 1) * MOBA_BLOCK, :]).astype(BF16)
        means = [jnp.sum(k[n * MOBA_BLOCK:(n + 1) * MOBA_BLOCK, :], axis=0, keepdims=True) * (1.0 / MOBA_BLOCK)
                 for n in range(nb)]
        ksum_ref[...] = jnp.concatenate(means + [jnp.zeros((8 - nb, d), F32)], axis=0)

    @pl.when(is_s)
    def _():
        ks_ref[...] = qkv[:, d:2 * d]
        vs_ref[...] = qkv[:, 2 * d:]


def _moba_qkv(x, mp4, ms3, layer, g_mix, w_qkv, n_p, seq_len, tile):
    n_all, d = x.shape
    n_s = n_all - n_p
    n_pt, n_st = n_p // tile, n_s // tile
    tps = seq_len // tile
    n_batch = n_p // seq_len
    tok = lambda cols: pl.BlockSpec((tile, cols), lambda i: (i, 0))
    in_specs = ([tok(d)] + _mod_specs(layer, 0, n_pt, tps, n_batch, d, tile)
                + _mod_specs(layer, 1, n_pt, tps, n_batch, d, tile) + [_full((1, d)), _full((d, 3 * d))])
    sp = _split_specs(n_pt, d, tile)
    assert tile % MOBA_BLOCK == 0 and tile // MOBA_BLOCK <= 8
    nb = tile // MOBA_BLOCK
    last = lambda i: jnp.minimum(i, n_pt - 1)
    outs = pl.pallas_call(
        functools.partial(_moba_qkv_kernel, n_pt, d),
        grid=(n_pt + n_st,),
        in_specs=in_specs,
        out_specs=[tok(d), sp[0], sp[1], sp[0], sp[1],
                   pl.BlockSpec((None, nb, d, MOBA_BLOCK), lambda i: (last(i) // tps, last(i) % tps, 0, 0)),
                   sp[0],
                   pl.BlockSpec((8, d), lambda i: (last(i), 0))],
        out_shape=(jax.ShapeDtypeStruct((n_all, d), F32),
                   jax.ShapeDtypeStruct((n_p, d), F32), jax.ShapeDtypeStruct((n_s, d), F32),
                   jax.ShapeDtypeStruct((n_p, d), F32), jax.ShapeDtypeStruct((n_s, d), F32),
                   jax.ShapeDtypeStruct((n_batch, seq_len // MOBA_BLOCK, d, MOBA_BLOCK), BF16),
                   jax.ShapeDtypeStruct((n_p, d), BF16),
                   jax.ShapeDtypeStruct((n_pt * 8, d), F32)),
        compiler_params=_cparams(("arbitrary",)),
        name="moba_qkv",
    )(x, mp4, ms3, mp4, ms3, g_mix.reshape(1, d), w_qkv.astype(BF16))
    q_all, k_p, k_s, v_p, v_s, k_b, v_b, ksum = outs
    kmean = ksum.reshape(n_pt, 8, d)[:, :nb, :].reshape(n_pt * nb, d)
    return q_all, k_p, k_s, v_p, v_s, k_b, v_b, kmean


def _t5_bias(dist, table_at):
    val = jnp.full(dist.shape, 0.0, F32) + table_at(0)
    for b in range(1, REL_BUCKETS):
        val = jnp.where(dist >= T5_THR[b], table_at(b), val)
    return val


MOBA_DROP = -32768.0


def _moba_prompt_kernel(q_ref, kt_ref, v_ref, mean_ref, tab_ref, o_ref,
                        bown_ref, bprev_ref, qa_ref, acc_ref, m_ref, l_ref):
    b = pl.program_id(0)
    i = pl.program_id(1)
    nh, dh, blk = MOBA_HEADS, MOBA_HEAD_DIM, MOBA_BLOCK
    nblk = kt_ref.shape[0]
    scale = dh ** -0.5
    ones = jnp.ones((blk, LANES), BF16)
    rep = lambda x: jnp.concatenate([x] * (blk // LANES), axis=-1)
    rowmax = lambda s: jnp.broadcast_to(jnp.max(s, axis=-1, keepdims=True), (blk, LANES))
    r = lax.broadcasted_iota(I32, (blk, blk), 0)
    c = lax.broadcasted_iota(I32, (blk, blk), 1)

    @pl.when(jnp.logical_and(b == 0, i == 0))
    def _():
        for h in range(nh):
            at = lambda bb, h=h: tab_ref[bb, h]
            bown_ref[h] = _t5_bias(jnp.maximum(r - c, 0), at)
            bprev_ref[h] = _t5_bias(blk + r - c, at)

    off = pl.multiple_of(i * blk, blk)
    bi = lax.broadcasted_iota(I32, (nblk, blk), 0)
    valid = bi < i
    for h in range(nh):
        hs = slice(h * dh, (h + 1) * dh)
        q = q_ref[:, hs]
        qb = q.astype(BF16)
        gate_t = jnp.where(valid, _dot_nt_f32(mean_ref[:, hs], q), NEG)
        rank = jnp.zeros(gate_t.shape, I32)
        for m_ in range(nblk - 1):
            gm = gate_t[m_:m_ + 1, :]
            ahead = jnp.logical_or(gm > gate_t, jnp.logical_and(gm == gate_t, m_ < bi))
            rank = rank + jnp.where(jnp.logical_and(ahead, m_ < i), 1, 0)
        drop_t = jnp.where(jnp.logical_and(valid, rank < MOBA_TOPK), 0.0, MOBA_DROP)
        drop = jnp.transpose(jnp.concatenate([drop_t, jnp.zeros((LANES - nblk, blk), F32)], axis=0))
        qa_ref[h] = jnp.concatenate([qb, drop.astype(BF16)], axis=-1)

        vo = jnp.concatenate([v_ref[pl.ds(off, blk), hs], ones], axis=-1)
        s = _dot(qb, kt_ref[i, hs, :]) * scale + bown_ref[h]
        s = jnp.where(c <= r, s, NEG)
        m = rowmax(s)
        p = jnp.exp(s - rep(m))
        pv = _dot(p.astype(BF16), vo)
        m_ref[h] = m
        l_ref[h] = pv[:, dh:]
        acc_ref[h] = pv[:, :dh]

    srow = lax.broadcasted_iota(I32, (LANES, blk), 0)

    def body(j, carry):
        offj = pl.multiple_of(j * blk, blk)
        onehot_t = jnp.where(srow == j, 1.0, 0.0).astype(BF16)
        is_prev = j == i - 1
        for h in range(nh):
            hs = slice(h * dh, (h + 1) * dh)
            kj = jnp.concatenate([kt_ref[j, hs, :], onehot_t], axis=0)
            vj = jnp.concatenate([v_ref[pl.ds(offj, blk), hs], ones], axis=-1)
            bias = jnp.where(is_prev, bprev_ref[h], tab_ref[REL_BUCKETS - 1, h])
            s = _dot(qa_ref[h], kj) * scale + bias
            m = m_ref[h]
            m_new = jnp.maximum(m, rowmax(s))
            alpha = jnp.exp(m - m_new)
            p = jnp.exp(s - rep(m_new))
            pv = _dot(p.astype(BF16), vj)
            m_ref[h] = m_new
            l_ref[h] = alpha * l_ref[h] + pv[:, dh:]
            acc_ref[h] = acc_ref[h] * alpha + pv[:, :dh]
        return carry

    lax.fori_loop(0, i, body, 0)
    for h in range(nh):
        o_ref[:, h * dh:(h + 1) * dh] = (acc_ref[h] / l_ref[h]).astype(o_ref.dtype)


def _moba_prompt_attn(q_all, k_b, v_b, kmean, rel_table, n_batch, seq_len):
    blk = MOBA_BLOCK
    nq = seq_len // blk
    dh = MOBA_HEAD_DIM
    nh = MOBA_HEADS
    d = nh * dh
    assert nq <= LANES and (nq % 8 == 0 or n_batch == 1) and dh == LANES
    return pl.pallas_call(
        _moba_prompt_kernel,
        grid=(n_batch, nq),
        in_specs=[pl.BlockSpec((blk, d), lambda b, i: (b * nq + i, 0)),
                  pl.BlockSpec((None, nq, d, blk), lambda b, i: (b, 0, 0, 0)),
                  pl.BlockSpec((seq_len, d), lambda b, i: (b, 0)),
                  pl.BlockSpec((nq, d), lambda b, i: (b, 0)),
                  pl.BlockSpec(memory_space=pltpu.SMEM)],
        out_specs=pl.BlockSpec((blk, d), lambda b, i: (b * nq + i, 0)),
        out_shape=jax.ShapeDtypeStruct((n_batch * seq_len, d), BF16),
        scratch_shapes=[pltpu.VMEM((nh, blk, blk), F32), pltpu.VMEM((nh, blk, blk), F32),
                        pltpu.VMEM((nh, blk, dh + LANES), BF16), pltpu.VMEM((nh, blk, dh), F32),
                        pltpu.VMEM((nh, blk, LANES), F32), pltpu.VMEM((nh, blk, LANES), F32)],
        compiler_params=_cparams(("arbitrary", "arbitrary")),
        name="moba_prompt_attn",
    )(q_all, k_b, v_b, kmean, rel_table)


def _moba_sample_kernel(layer, n_pages, page, t_new, cpages,
                        pt_ref, q_ref, kn_ref, vn_ref, trow_ref, k_hbm, v_hbm, o_ref,
                        buf, sum_ref, s_buf, sem):
    b = pl.program_id(0)
    nb = pl.num_programs(0)
    nh, dh, blk = MOBA_HEADS, MOBA_HEAD_DIM, MOBA_BLOCK
    rows = nh * t_new
    past = n_pages * page
    chunk = cpages * page
    n_chunks = n_pages // cpages
    nblk = past // blk
    bpc = chunk // blk
    scale = dh ** -0.5

    def start_chunk(seq, ci, src, sl):
        def body(j, carry):
            pid = pt_ref[seq, ci * cpages + j]
            dst_rows = pl.ds(pl.multiple_of(j * page, page), page)
            for h in range(nh):
                pltpu.make_async_copy(src.at[layer, pid, :, h, :], buf.at[sl, h, dst_rows, :], sem.at[sl]).start()
            return carry
        lax.fori_loop(0, cpages, body, 0)

    def wait_chunk(src, sl):
        for _ in range(cpages * nh):
            pltpu.make_async_copy(src.at[layer, 0, :, 0, :], buf.at[sl, 0, pl.ds(0, page), :], sem.at[sl]).wait()

    @pl.when(b == 0)
    def _():
        for ci in range(min(2, n_chunks)):
            start_chunk(0, ci, k_hbm, ci)
        buf[2] = jnp.zeros(buf.shape[1:], buf.dtype)
        buf[3] = jnp.zeros(buf.shape[1:], buf.dtype)

    q = q_ref[...]
    qh = [q[:, h * dh:(h + 1) * dh] for h in range(nh)]
    qh_b = [x.astype(BF16) for x in qh]

    for ci in range(n_chunks):
        sl = ci % 2
        wait_chunk(k_hbm, sl)
        for h in range(nh):
            kf = buf[sl, h]
            sum_ref[h, ci * bpc:(ci + 1) * bpc, :] = jnp.sum(kf.reshape(bpc, blk, dh), axis=1)
            s_buf[h * t_new:(h + 1) * t_new, ci * chunk:(ci + 1) * chunk] = _dot_nt(qh_b[h], kf.astype(BF16)) * scale
        if ci + 2 < n_chunks:
            start_chunk(b, ci + 2, k_hbm, sl)

    @pl.when(b + 1 < nb)
    def _():
        for ci in range(min(2, n_chunks)):
            start_chunk(b + 1, ci, k_hbm, ci)

    gate = jnp.concatenate(
        [_dot_nt_f32(qh[h], sum_ref[h] * (1.0 / blk)) for h in range(nh)], axis=0)
    bl = lax.broadcasted_iota(I32, gate.shape, 1)
    big = jnp.int32(1 << 20)
    sel = jnp.zeros(gate.shape, F32)
    g = gate
    for _ in range(min(MOBA_TOPK, nblk)):
        gm = jnp.max(g, axis=-1, keepdims=True)
        gi = jnp.min(jnp.where(g == gm, bl, big), axis=-1, keepdims=True)
        hit = bl == gi
        sel = jnp.where(hit, 1.0, sel)
        g = jnp.where(hit, NEG * 2.0, g)

    lane1 = lax.broadcasted_iota(I32, (1, nblk), 1)
    pw = jnp.left_shift(1, lane1 & 15).astype(F32)
    words = []
    for h in range(nh):
        need_h = jnp.max(sel[h * t_new:(h + 1) * t_new, :], axis=0, keepdims=True) * pw
        words.append([jnp.sum(jnp.where((lane1 >> 4) == w, need_h, 0.0)).astype(I32)
                      for w in range((nblk + 15) // 16)])
    ppb = blk // page

    def v_copies(ci, sl, n, h):
        out = []
        for pg in range(ppb):
            j = n * ppb + pg
            pid = pt_ref[b, ci * cpages + j]
            out.append(pltpu.make_async_copy(v_hbm.at[layer, pid, :, h, :],
                                             buf.at[sl, h, pl.ds(j * page, page), :], sem.at[sl]))
        return out

    def for_needed(ci, fn):
        for n in range(bpc):
            g_blk = ci * bpc + n
            for h in range(nh):
                @pl.when(((words[h][g_blk // 16] >> (g_blk % 16)) & 1) == 1)
                def _(n=n, h=h):
                    fn(n, h)

    def start_v(ci, sl):
        for_needed(ci, lambda n, h: [cp.start() for cp in v_copies(ci, sl, n, h)])

    def wait_v(ci, sl):
        for_needed(ci, lambda n, h: [cp.wait() for cp in v_copies(ci, sl, n, h)])

    for ci in range(min(2, n_chunks)):
        start_v(ci, 2 + ci)
    kpos_blk = lax.broadcasted_iota(I32, (nblk, past), 1) // blk
    expand = jnp.where(kpos_blk == lax.broadcasted_iota(I32, (nblk, past), 0), 1.0, 0.0).astype(BF16)
    keep = _dot(sel.astype(BF16), expand)

    trow = trow_ref[...]
    col = lambda b_: trow[:, b_:b_ + 1]
    t_of_row = lax.broadcasted_iota(I32, (rows, LANES), 0) & (t_new - 1)
    kpos_tail = (past - LANES) + lax.broadcasted_iota(I32, (rows, LANES), 1)
    bias_tail = _t5_bias(past + t_of_row - kpos_tail, col)
    s = s_buf[...] + col(REL_BUCKETS - 1)
    s_tail = s_buf[:, past - LANES:] + bias_tail
    s = jnp.concatenate([s[:, :past - LANES], s_tail], axis=-1)
    s = jnp.where(keep > 0.5, s, NEG)

    kn = kn_ref[...].astype(BF16)
    vn = vn_ref[...].astype(BF16)
    s_own = jnp.concatenate([_dot_nt(qh_b[h], kn[:, h * dh:(h + 1) * dh]) for h in range(nh)], axis=0) * scale
    r_t = lax.broadcasted_iota(I32, s_own.shape, 0) & (t_new - 1)
    c_t = lax.broadcasted_iota(I32, s_own.shape, 1)
    s_own = s_own + _t5_bias(jnp.maximum(r_t - c_t, 0), col)
    s_own = jnp.where(c_t <= r_t, s_own, NEG)

    m = jnp.maximum(jnp.max(s, axis=-1, keepdims=True), jnp.max(s_own, axis=-1, keepdims=True))
    p = jnp.exp(s - m)
    p_own = jnp.exp(s_own - m)
    l = jnp.sum(p, axis=-1, keepdims=True) + jnp.sum(p_own, axis=-1, keepdims=True)
    s_buf[...] = p
    p_own = p_own.astype(BF16)
    acc = [_dot(p_own[h * t_new:(h + 1) * t_new, :], vn[:, h * dh:(h + 1) * dh]) for h in range(nh)]

    for ci in range(n_chunks):
        sl = 2 + ci % 2
        wait_v(ci, sl)
        for h in range(nh):
            vh = buf[sl, h].astype(BF16)
            ph = s_buf[h * t_new:(h + 1) * t_new, ci * chunk:(ci + 1) * chunk].astype(BF16)
            acc[h] = acc[h] + _dot(ph, vh)
        if ci + 2 < n_chunks:
            start_v(ci + 2, sl)
    for h in range(nh):
        o_ref[:, h * dh:(h + 1) * dh] = acc[h] / l[h * t_new:(h + 1) * t_new, :]


def _moba_sample_attn(page_table, q_all, k_s, v_s, rel_table, cache_k, cache_v, layer, n_p, t_new):
    n_seq, n_pages = page_table.shape
    n_layers, n_phys, page, nh, dh = cache_k.shape
    d = nh * dh
    past = n_pages * page
    cpages = min(8, n_pages // 2)
    assert n_pages % cpages == 0 and (cpages * page) % MOBA_BLOCK == 0
    rows = nh * t_new
    q_off = n_p // t_new
    trow = jnp.repeat(rel_table.T, t_new, axis=0)
    grid_spec = pltpu.PrefetchScalarGridSpec(
        num_scalar_prefetch=1,
        grid=(n_seq,),
        in_specs=[pl.BlockSpec((t_new, d), lambda b, pt: (q_off + b, 0)),
                  pl.BlockSpec((t_new, d), lambda b, pt: (b, 0)),
                  pl.BlockSpec((t_new, d), lambda b, pt: (b, 0)),
                  pl.BlockSpec((rows, REL_BUCKETS), lambda b, pt: (0, 0)),
                  pl.BlockSpec(memory_space=pl.ANY),
                  pl.BlockSpec(memory_space=pl.ANY)],
        out_specs=pl.BlockSpec((t_new, d), lambda b, pt: (b, 0)),
        scratch_shapes=[pltpu.VMEM((4, nh, cpages * page, dh), F32),
                        pltpu.VMEM((nh, past // MOBA_BLOCK, dh), F32),
                        pltpu.VMEM((rows, past), F32),
                        pltpu.SemaphoreType.DMA((4,))])
    return pl.pallas_call(
        functools.partial(_moba_sample_kernel, layer, n_pages, page, t_new, cpages),
        grid_spec=grid_spec,
        out_shape=jax.ShapeDtypeStruct((n_seq * t_new, d), F32),
        compiler_params=_cparams(("arbitrary",)),
        name="moba_sample_attn",
    )(page_table, q_all, k_s, v_s, trow, cache_k, cache_v)


def _rope_tables(pos):
    half = MLA_ROPE // 2
    inv_freq = ROPE_THETA ** (-jnp.arange(half, dtype=F32) / half)
    ang = pos.astype(F32)[:, None] * inv_freq
    z = jnp.zeros((pos.shape[0], LANES - MLA_ROPE), F32)
    cos = jnp.concatenate([jnp.cos(ang), jnp.cos(ang), z], axis=-1)
    sin = jnp.concatenate([jnp.sin(ang), jnp.sin(ang), z], axis=-1)
    return cos, sin


def kernel(x_prompt, x_sample, cache_mla_ckv, cache_mla_kpe, cache_moba_k, cache_moba_v, page_table, c_prompt, c_sample, w_ada, b_ada, g_norm_mix, g_norm_ffn, g_final, w_dq, g_q, w_uq, w_dkv, g_kv, w_uk, w_uv, w_o_mla, w_qkv_moba, w_o_moba, rel_bias_table, w_router_group, w_router_expert, w_exp_gate, w_exp_up, w_exp_down):
    n_batch, seq_len, d = x_prompt.shape
    n_seq, t_new, _ = x_sample.shape
    depth = w_ada.shape[0]
    page = cache_mla_ckv.shape[2]
    past = page_table.shape[1] * page
    n_p, n_s = n_batch * seq_len, n_seq * t_new
    tile = min(TOK_TILE, n_s)
    assert n_s % tile == 0 and seq_len % tile == 0 and tile % t_new == 0

    xp = x_prompt.reshape(n_p, d)
    xs = x_sample.reshape(n_s, d)
    m = _ada(jnp.concatenate([c_sample, c_prompt], axis=0), w_ada, b_ada)
    mp4 = m[:, n_seq:, :].reshape(depth, n_batch, 1, 6 * d)
    ms3 = jnp.repeat(m[:, :n_seq, :], t_new, axis=1)

    cos_p, sin_p = _rope_tables(jnp.arange(seq_len))
    cos_s, sin_s = _rope_tables(past + (jnp.arange(tile) % t_new))
    cos_all = jnp.concatenate([cos_p, cos_s], axis=0)
    sin_all = jnp.concatenate([sin_p, sin_s], axis=0)

    outs = {}
    x = (xp, xs)
    for i in range(depth):
        if i % 2 == 0:
            la = i // 2
            if not isinstance(x, tuple):
                x = (x[:n_p], x[n_p:])
            ckv_p, ckv_s, kpe_p, kpe_s, q_all, k_all, q_s, k_s = _mla_proj(
                x[0], x[1], mp4, ms3, i, g_norm_mix[i], cos_all, sin_all,
                w_dq[la], g_q[la], w_uq[la], w_dkv[la], g_kv[la], w_uk[la], seq_len, tile)
            o_p = _mla_prompt_attn(q_all, k_all, n_batch, seq_len, min(ATT_TILE, seq_len))
            o_s = _mla_sample_attn(page_table, q_s, k_s, cache_mla_ckv, cache_mla_kpe, la, t_new)
            outs.setdefault("ckv_p", []).append(ckv_p.reshape(n_batch, seq_len, MLA_KV_RANK))
            outs.setdefault("kpe_p", []).append(kpe_p.reshape(n_batch, seq_len, MLA_ROPE))
            outs.setdefault("ckv_s", []).append(ckv_s.reshape(n_seq, t_new, MLA_KV_RANK))
            outs.setdefault("kpe_s", []).append(kpe_s.reshape(n_seq, t_new, MLA_ROPE))
            x_new, h2, route = _attn_out(o_p, o_s, x, mp4, ms3, i, g_norm_ffn[i], w_uv[la], w_o_mla[la],
                                         w_router_group[i], w_router_expert[i], seq_len, tile)
        else:
            lb = i // 2
            xu = x if not isinstance(x, tuple) else jnp.concatenate(x, axis=0)
            q_all, k_p, k_s, v_p, v_s, k_b, v_b, kmean = _moba_qkv(
                xu, mp4, ms3, i, g_norm_mix[i], w_qkv_moba[lb], n_p, seq_len, tile)
            o_p = _moba_prompt_attn(q_all, k_b, v_b, kmean, rel_bias_table, n_batch, seq_len)
            o_s = _moba_sample_attn(page_table, q_all, k_s, v_s, rel_bias_table, cache_moba_k, cache_moba_v,
                                    lb, n_p, t_new)
            hd = (MOBA_HEADS, MOBA_HEAD_DIM)
            outs.setdefault("k_p", []).append(k_p.reshape(n_batch, seq_len, *hd))
            outs.setdefault("v_p", []).append(v_p.reshape(n_batch, seq_len, *hd))
            outs.setdefault("k_s", []).append(k_s.reshape(n_seq, t_new, *hd))
            outs.setdefault("v_s", []).append(v_s.reshape(n_seq, t_new, *hd))
            x_new, h2, route = _attn_out(o_p, o_s, xu, mp4, ms3, i, g_norm_ffn[i], None, w_o_moba[lb],
                                         w_router_group[i], w_router_expert[i], seq_len, tile)
        y2 = _moe(h2, route, w_exp_gate, w_exp_up, w_exp_down, i, MOE_TILE)
        last = i == depth - 1
        x = _combine(x_new, y2, mp4, ms3, i, g_final if last else None, n_p, seq_len, tile)
    y_p, y_s = x
    st = lambda key: jnp.stack(outs[key])
    return (y_p.reshape(n_batch, seq_len, d), y_s.reshape(n_seq, t_new, d),
            st("ckv_p"), st("kpe_p"), st("k_p"), st("v_p"), st("ckv_s"), st("kpe_s"), st("k_s"), st("v_s"))
```

```python
import functools
import math

import numpy as np
import jax
import jax.numpy as jnp
from jax import lax
from jax.experimental import pallas as pl
from jax.experimental.pallas import tpu as pltpu

F32 = jnp.float32
BF16 = jnp.bfloat16
I32 = jnp.int32

EPS = 1e-6
ROPE_THETA = 10000.0
MLA_HEADS = 8
MLA_NOPE = 128
MLA_ROPE = 64
MLA_KV_RANK = 256
MLA_V = 128
MLA_SCALE = (MLA_NOPE + MLA_ROPE) ** -0.5
MLA_QK = MLA_KV_RANK + 128
MOBA_HEADS = 8
MOBA_HEAD_DIM = 128
MOBA_BLOCK = 256
MOBA_TOPK = 3
REL_BUCKETS = 32
REL_MAX_DIST = 128
N_GROUPS = 4
EXPERTS_PER_GROUP = 8
N_EXPERTS = N_GROUPS * EXPERTS_PER_GROUP
LANES = 128
NEG = -1e30

TOK_TILE = 512
ATT_TILE = 256
MOE_TILE = 256
VMEM_LIMIT = 56 * 1024 * 1024


def _t5_thresholds():
    max_exact = REL_BUCKETS // 2
    d = np.arange(1, 4 * REL_MAX_DIST).astype(np.float32)
    t = (np.log(d / np.float32(max_exact)) / np.float32(math.log(REL_MAX_DIST / max_exact))
         * np.float32(REL_BUCKETS - max_exact)).astype(np.float32)
    b = np.where(d < max_exact, d.astype(np.int32), np.minimum(max_exact + t.astype(np.int32), REL_BUCKETS - 1))
    return [int(d[np.argmax(b >= k)]) for k in range(REL_BUCKETS)]


T5_THR = _t5_thresholds()


def _cparams(sem):
    return pltpu.CompilerParams(dimension_semantics=sem, vmem_limit_bytes=VMEM_LIMIT)


def _rms(x, g):
    return x * lax.rsqrt(jnp.mean(x * x, axis=-1, keepdims=True) + EPS) * g


def _sigmoid(x):
    return 1.0 / (1.0 + jnp.exp(-x))


def _dot(a, b):
    return jnp.dot(a, b, preferred_element_type=F32)


def _dot_nt(a, b):
    return lax.dot_general(a, b, (((1,), (1,)), ((), ())), preferred_element_type=F32)


def _dot_f32(a, b):
    return jnp.dot(a, b, preferred_element_type=F32, precision=lax.Precision.HIGHEST)


def _dot_nt_f32(a, b):
    return lax.dot_general(a, b, (((1,), (1,)), ((), ())), preferred_element_type=F32,
                           precision=lax.Precision.HIGHEST)


def _split_specs(n_pt, cols, tile):
    return [pl.BlockSpec((tile, cols), lambda i: (jnp.minimum(i, n_pt - 1), 0)),
            pl.BlockSpec((tile, cols), lambda i: (jnp.maximum(i - n_pt, 0), 0))]


def _mod_specs(layer, k, n_pt, tiles_per_seq, n_batch, d, tile):
    return [pl.BlockSpec((None, None, 1, d),
                         lambda i: (layer, jnp.minimum(i // tiles_per_seq, n_batch - 1), 0, k)),
            pl.BlockSpec((None, tile, d), lambda i: (layer, jnp.maximum(i - n_pt, 0), k))]


def _full(shape):
    nd = len(shape)
    return pl.BlockSpec(shape, lambda *_: (0,) * nd)


def _ada_kernel(c_ref, w_ref, b_ref, o_ref):
    c = c_ref[...]
    s = c * _sigmoid(c)
    o_ref[...] = _dot(s.astype(BF16), w_ref[...].astype(BF16)) + b_ref[...]


def _ada(c_all, w_ada, b_ada):
    depth, d, d6 = w_ada.shape
    n = c_all.shape[0]
    cols = 1536
    return pl.pallas_call(
        _ada_kernel,
        grid=(depth, d6 // cols),
        in_specs=[pl.BlockSpec((n, d), lambda l, j: (0, 0)),
                  pl.BlockSpec((None, d, cols), lambda l, j: (l, 0, j)),
                  pl.BlockSpec((None, 1, cols), lambda l, j: (l, 0, j))],
        out_specs=pl.BlockSpec((None, n, cols), lambda l, j: (l, 0, j)),
        out_shape=jax.ShapeDtypeStruct((depth, n, d6), F32),
        compiler_params=_cparams(("arbitrary", "arbitrary")),
        name="ada_params",
    )(c_all, w_ada, b_ada.reshape(depth, 1, d6))


def _mla_proj_kernel(n_pt, xp_ref, xs_ref, shp_ref, shs_ref, scp_ref, scs_ref, gmix_ref, cos_ref, sin_ref,
                     wdq_ref, gq_ref, wuqn_ref, wuqp_ref, wuqr_ref, wuk_ref, wdkv_ref, wdkvr_ref, gkv_ref,
                     ckvp_ref, ckvs_ref, kpep_ref, kpes_ref, q_ref, k_ref, qs_ref, ks_ref):
    i = pl.program_id(0)
    is_s = i >= n_pt
    x = jnp.where(is_s, xs_ref[...], xp_ref[...])
    shift = jnp.where(is_s, shs_ref[...], shp_ref[...])
    scale = jnp.where(is_s, scs_ref[...], scp_ref[...])
    h = (_rms(x, gmix_ref[...]) * (1.0 + scale) + shift).astype(BF16)
    cos = cos_ref[...]
    sin = sin_ref[...]

    cq = _dot(h, wdq_ref[...])
    cqn = _rms(cq, gq_ref[...]).astype(BF16)
    qn = _dot(cqn, wuqn_ref[...]).astype(BF16)
    qpe = _dot(cqn, wuqp_ref[...])
    qrot = _dot(cqn, wuqr_ref[...])
    kv = _dot(h, wdkv_ref[...])
    kvrot = _dot(h, wdkvr_ref[...])
    ckv = _rms(kv[:, :MLA_KV_RANK], gkv_ref[...])
    kpe = kv[:, MLA_KV_RANK:] * cos + kvrot * sin

    pieces = []
    for hh in range(MLA_HEADS):
        sl = slice(hh * LANES, (hh + 1) * LANES)
        pieces.append(_dot(qn[:, sl], wuk_ref[hh]))
        pieces.append(qpe[:, sl] * cos + qrot[:, sl] * sin)
    q = jnp.concatenate(pieces, axis=-1)
    k = jnp.concatenate([ckv, kpe], axis=-1)
    q_ref[...] = q.astype(BF16)
    k_ref[...] = k.astype(BF16)

    @pl.when(jnp.logical_not(is_s))
    def _():
        ckvp_ref[...] = ckv
        kpep_ref[...] = kpe[:, :MLA_ROPE]

    @pl.when(is_s)
    def _():
        ckvs_ref[...] = ckv
        kpes_ref[...] = kpe[:, :MLA_ROPE]
        qs_ref[...] = q
        ks_ref[...] = k


def _rot_cols(w):
    half = MLA_ROPE // 2
    return jnp.concatenate([-w[..., half:], w[..., :half]], axis=-1)


def _mla_proj(xp, xs, mp4, ms3, layer, g_mix, cos_all, sin_all, w_dq, g_q, w_uq, w_dkv, g_kv, w_uk,
              seq_len, tile):
    n_p, d = xp.shape
    n_s = xs.shape[0]
    n_pt, n_st = n_p // tile, n_s // tile
    tps = seq_len // tile
    n_batch = n_p // seq_len
    qr = w_dq.shape[1]
    hq = MLA_HEADS * LANES
    wq = w_uq.reshape(qr, MLA_HEADS, MLA_NOPE + MLA_ROPE)
    w_n = wq[:, :, :MLA_NOPE].reshape(qr, hq).astype(BF16)
    zpad = jnp.zeros((qr, MLA_HEADS, LANES - MLA_ROPE), F32)
    w_p = jnp.concatenate([wq[:, :, MLA_NOPE:], zpad], axis=-1).reshape(qr, hq).astype(BF16)
    w_r = jnp.concatenate([_rot_cols(wq[:, :, MLA_NOPE:]), zpad], axis=-1).reshape(qr, hq).astype(BF16)
    w_ukt = jnp.transpose(w_uk, (1, 2, 0)).astype(BF16)
    zk = jnp.zeros((d, LANES - MLA_ROPE), F32)
    w_kv = jnp.concatenate([w_dkv, zk], axis=-1).astype(BF16)
    w_kvr = jnp.concatenate([_rot_cols(w_dkv[:, MLA_KV_RANK:]), zk], axis=-1).astype(BF16)
    n_tab = cos_all.shape[0] // tile - 1

    tab_spec = pl.BlockSpec((tile, LANES), lambda i: (jnp.where(i < n_pt, i % tps, n_tab), 0))
    in_specs = (_split_specs(n_pt, d, tile)
                + _mod_specs(layer, 0, n_pt, tps, n_batch, d, tile)
                + _mod_specs(layer, 1, n_pt, tps, n_batch, d, tile)
                + [_full((1, d)), tab_spec, tab_spec,
                   _full((d, qr)), _full((1, qr)), _full((qr, hq)), _full((qr, hq)), _full((qr, hq)),
                   _full((MLA_HEADS, MLA_NOPE, MLA_KV_RANK)), _full((d, MLA_QK)), _full((d, LANES)),
                   _full((1, MLA_KV_RANK))])
    n_all = n_p + n_s
    out_shape = (jax.ShapeDtypeStruct((n_p, MLA_KV_RANK), F32), jax.ShapeDtypeStruct((n_s, MLA_KV_RANK), F32),
                 jax.ShapeDtypeStruct((n_p, MLA_ROPE), F32), jax.ShapeDtypeStruct((n_s, MLA_ROPE), F32),
                 jax.ShapeDtypeStruct((n_all, MLA_HEADS * MLA_QK), BF16),
                 jax.ShapeDtypeStruct((n_all, MLA_QK), BF16),
                 jax.ShapeDtypeStruct((n_s, MLA_HEADS * MLA_QK), F32),
                 jax.ShapeDtypeStruct((n_s, MLA_QK), F32))
    out_specs = (_split_specs(n_pt, MLA_KV_RANK, tile) + _split_specs(n_pt, MLA_ROPE, tile)
                 + [pl.BlockSpec((tile, MLA_HEADS * MLA_QK), lambda i: (i, 0)),
                    pl.BlockSpec((tile, MLA_QK), lambda i: (i, 0)),
                    _split_specs(n_pt, MLA_HEADS * MLA_QK, tile)[1],
                    _split_specs(n_pt, MLA_QK, tile)[1]])
    return pl.pallas_call(
        functools.partial(_mla_proj_kernel, n_pt),
        grid=(n_pt + n_st,),
        in_specs=in_specs, out_specs=out_specs, out_shape=out_shape,
        compiler_params=_cparams(("arbitrary",)),
        name="mla_proj",
    )(xp, xs, mp4, ms3, mp4, ms3, g_mix.reshape(1, d), cos_all, sin_all,
      w_dq.astype(BF16), g_q.reshape(1, qr), w_n, w_p, w_r, w_ukt, w_kv, w_kvr, g_kv.reshape(1, MLA_KV_RANK))


def _mla_prompt_kernel(tq, q_ref, k_ref, o_ref, acc_ref):
    i = pl.program_id(1)
    m_rows = MLA_HEADS * tq
    q = q_ref[...]
    qs = jnp.concatenate([q[:, h * MLA_QK:(h + 1) * MLA_QK] for h in range(MLA_HEADS)], axis=0)

    def step(kblk, m, l, masked):
        s = _dot_nt(qs, kblk) * MLA_SCALE
        if masked:
            r = lax.broadcasted_iota(I32, s.shape, 0) & (tq - 1)
            c = lax.broadcasted_iota(I32, s.shape, 1)
            s = jnp.where(c <= r, s, NEG)
        m_new = jnp.maximum(m, jnp.max(s, axis=-1, keepdims=True))
        alpha = jnp.exp(m - m_new)
        p = jnp.exp(s - m_new)
        l_new = alpha * l + jnp.sum(p, axis=-1, keepdims=True)
        pv = _dot(p.astype(BF16), kblk[:, :MLA_KV_RANK])
        return m_new, l_new, alpha, pv

    kd = k_ref[pl.ds(pl.multiple_of(i * tq, tq), tq), :]
    m0 = jnp.full((m_rows, 1), NEG, F32)
    l0 = jnp.zeros((m_rows, 1), F32)
    m, l, _, pv = step(kd, m0, l0, True)
    acc_ref[...] = pv

    def body(j, carry):
        m, l = carry
        kb = k_ref[pl.ds(pl.multiple_of(j * tq, tq), tq), :]
        m, l, alpha, pv = step(kb, m, l, False)
        acc_ref[...] = acc_ref[...] * alpha + pv
        return m, l

    m, l = lax.fori_loop(0, i, body, (m, l))
    o = acc_ref[...] / l
    for h in range(MLA_HEADS):
        o_ref[:, h * MLA_KV_RANK:(h + 1) * MLA_KV_RANK] = o[h * tq:(h + 1) * tq, :].astype(o_ref.dtype)


def _mla_prompt_attn(q_all, k_all, n_batch, seq_len, tq):
    nq = seq_len // tq
    return pl.pallas_call(
        functools.partial(_mla_prompt_kernel, tq),
        grid=(n_batch, nq),
        in_specs=[pl.BlockSpec((tq, MLA_HEADS * MLA_QK), lambda b, i: (b * nq + i, 0)),
                  pl.BlockSpec((seq_len, MLA_QK), lambda b, i: (b, 0))],
        out_specs=pl.BlockSpec((tq, MLA_HEADS * MLA_KV_RANK), lambda b, i: (b * nq + i, 0)),
        out_shape=jax.ShapeDtypeStruct((n_batch * seq_len, MLA_HEADS * MLA_KV_RANK), BF16),
        scratch_shapes=[pltpu.VMEM((MLA_HEADS * tq, MLA_KV_RANK), F32)],
        compiler_params=_cparams(("arbitrary", "arbitrary")),
        name="mla_prompt_attn",
    )(q_all, k_all)


def _mla_sample_kernel(layer, n_pages, page, t_new, chunk,
                       pt_ref, q_ref, k_ref, ckv_hbm, kpe_hbm, o_ref,
                       ckv_buf, kpe_buf, kb_buf, s_buf, sem):
    b = pl.program_id(0)
    nb = pl.num_programs(0)
    slot = b % 2
    past = n_pages * page

    def copies(seq, sl):
        out = []
        for j in range(n_pages):
            pid = pt_ref[seq, j]
            out.append(pltpu.make_async_copy(ckv_hbm.at[layer, pid], ckv_buf.at[sl, pl.ds(j * page, page)], sem.at[0, sl]))
            out.append(pltpu.make_async_copy(kpe_hbm.at[layer, pid], kpe_buf.at[sl, j], sem.at[1, sl]))
        return out

    @pl.when(b == 0)
    def _():
        for cp in copies(0, 0):
            cp.start()

    @pl.when(b + 1 < nb)
    def _():
        for cp in copies(b + 1, 1 - slot):
            cp.start()

    for cp in copies(b, slot):
        cp.wait()

    q = q_ref[...]
    qs = jnp.concatenate([q[:, h * MLA_QK:(h + 1) * MLA_QK] for h in range(MLA_HEADS)], axis=0).astype(BF16)
    ql = qs[:, :MLA_KV_RANK]
    qp = qs[:, MLA_KV_RANK:MLA_KV_RANK + MLA_ROPE]
    ppc = chunk // page
    for c in range(past // chunk):
        kc = ckv_buf[slot, c * chunk:(c + 1) * chunk, :].astype(BF16)
        kb_buf[c * chunk:(c + 1) * chunk, :] = kc
        kpe_c = jnp.concatenate([kpe_buf[slot, c * ppc + j] for j in range(ppc)], axis=-1).astype(BF16)
        s_pe = _dot(qp, kpe_c)
        s_buf[:, c * chunk:(c + 1) * chunk] = (_dot_nt(ql, kc) + s_pe) * MLA_SCALE

    kn = k_ref[...].astype(BF16)
    s_new = _dot_nt(qs, kn) * MLA_SCALE
    r = lax.broadcasted_iota(I32, s_new.shape, 0) & (t_new - 1)
    c_ = lax.broadcasted_iota(I32, s_new.shape, 1)
    s_new = jnp.where(c_ <= r, s_new, NEG)
    s = s_buf[...]
    m = jnp.maximum(jnp.max(s, axis=-1, keepdims=True), jnp.max(s_new, axis=-1, keepdims=True))
    p_new = jnp.exp(s_new - m)
    l = jnp.sum(p_new, axis=-1, keepdims=True)
    acc = _dot(p_new.astype(BF16), kn[:, :MLA_KV_RANK])
    for c in range(past // chunk):
        p = jnp.exp(s_buf[:, c * chunk:(c + 1) * chunk] - m)
        l = l + jnp.sum(p, axis=-1, keepdims=True)
        acc = acc + _dot(p.astype(BF16), kb_buf[c * chunk:(c + 1) * chunk, :])
    o = acc / l
    for h in range(MLA_HEADS):
        o_ref[:, h * MLA_KV_RANK:(h + 1) * MLA_KV_RANK] = o[h * t_new:(h + 1) * t_new, :]


def _mla_sample_attn(page_table, q_s, k_s, cache_ckv, cache_kpe, layer, t_new):
    n_seq, n_pages = page_table.shape
    page = cache_ckv.shape[2]
    past = n_pages * page
    chunk = min(1024, past)
    rows = MLA_HEADS * t_new
    cache_kpe = jnp.swapaxes(cache_kpe, 2, 3)
    grid_spec = pltpu.PrefetchScalarGridSpec(
        num_scalar_prefetch=1,
        grid=(n_seq,),
        in_specs=[pl.BlockSpec((t_new, MLA_HEADS * MLA_QK), lambda b, pt: (b, 0)),
                  pl.BlockSpec((t_new, MLA_QK), lambda b, pt: (b, 0)),
                  pl.BlockSpec(memory_space=pl.ANY),
                  pl.BlockSpec(memory_space=pl.ANY)],
        out_specs=pl.BlockSpec((t_new, MLA_HEADS * MLA_KV_RANK), lambda b, pt: (b, 0)),
        scratch_shapes=[pltpu.VMEM((2, past, MLA_KV_RANK), F32),
                        pltpu.VMEM((2, n_pages, MLA_ROPE, page), F32),
                        pltpu.VMEM((past, MLA_KV_RANK), BF16),
                        pltpu.VMEM((rows, past), F32),
                        pltpu.SemaphoreType.DMA((2, 2))])
    return pl.pallas_call(
        functools.partial(_mla_sample_kernel, layer, n_pages, page, t_new, chunk),
        grid_spec=grid_spec,
        out_shape=jax.ShapeDtypeStruct((n_seq * t_new, MLA_HEADS * MLA_KV_RANK), F32),
        compiler_params=_cparams(("arbitrary",)),
        name="mla_sample_attn",
    )(page_table, q_s, k_s, cache_ckv, cache_kpe)


def _route(z):
    lane = lax.broadcasted_iota(I32, z.shape, 1)
    big = jnp.int32(1 << 20)
    gmask = lane < N_GROUPS
    zg = jnp.where(gmask, z, NEG)
    gmax = jnp.max(zg, axis=-1, keepdims=True)
    gidx = jnp.min(jnp.where(jnp.logical_and(gmask, zg == gmax), lane, big), axis=-1, keepdims=True)
    gsum = jnp.sum(jnp.where(gmask, jnp.exp(zg - gmax), 0.0), axis=-1, keepdims=True)
    g_top = 1.0 / gsum
    lo = N_GROUPS + gidx * EXPERTS_PER_GROUP
    emask = jnp.logical_and(lane >= lo, lane < lo + EXPERTS_PER_GROUP)
    z1 = jnp.where(emask, z, NEG)
    e1 = jnp.max(z1, axis=-1, keepdims=True)
    i1 = jnp.min(jnp.where(jnp.logical_and(emask, z1 == e1), lane, big), axis=-1, keepdims=True)
    z2 = jnp.where(lane == i1, NEG, z1)
    e2 = jnp.max(z2, axis=-1, keepdims=True)
    i2 = jnp.min(jnp.where(jnp.logical_and(emask, jnp.logical_and(z2 == e2, lane != i1)), lane, big),
                 axis=-1, keepdims=True)
    t = jnp.exp(e2 - e1)
    w1 = g_top / (1.0 + t)
    w2 = g_top * t / (1.0 + t)
    out = jnp.where(lane == 0, (i1 - N_GROUPS).astype(F32),
                    jnp.where(lane == 1, (i2 - N_GROUPS).astype(F32),
                              jnp.where(lane == 2, w1, jnp.where(lane == 3, w2, 0.0))))
    return out


def _attn_out_kernel(n_pt, split_x, use_uv, *refs):
    refs = list(refs)
    op_ref, os_ref = refs[:2]
    refs = refs[2:]
    if split_x:
        xp_ref, xs_ref = refs[:2]
        refs = refs[2:]
    else:
        x_ref = refs[0]
        refs = refs[1:]
    (gp_ref, gs_ref, shp_ref, shs_ref, scp_ref, scs_ref, gffn_ref) = refs[:7]
    refs = refs[7:]
    if use_uv:
        wuv_ref = refs[0]
        refs = refs[1:]
    wo_ref, wrh_ref, wrl_ref, xo_ref, h2_ref, rt_ref = refs

    i = pl.program_id(0)
    is_s = i >= n_pt
    o = jnp.where(is_s, os_ref[...].astype(BF16), op_ref[...])
    if split_x:
        x = jnp.where(is_s, xs_ref[...], xp_ref[...])
    else:
        x = x_ref[...]
    gate = jnp.where(is_s, gs_ref[...], gp_ref[...])
    shift = jnp.where(is_s, shs_ref[...], shp_ref[...])
    scale = jnp.where(is_s, scs_ref[...], scp_ref[...])
    if use_uv:
        o = jnp.concatenate(
            [_dot(o[:, h * MLA_KV_RANK:(h + 1) * MLA_KV_RANK], wuv_ref[h]) for h in range(MLA_HEADS)],
            axis=-1).astype(BF16)
    a = _dot(o, wo_ref[...])
    xn = x + gate * a
    xo_ref[...] = xn
    h2 = _rms(xn, gffn_ref[...]) * (1.0 + scale) + shift
    h2_ref[...] = h2
    h2h = h2.astype(BF16)
    h2l = (h2 - h2h.astype(F32)).astype(BF16)
    z = _dot(h2h, wrh_ref[...]) + (_dot(h2h, wrl_ref[...]) + _dot(h2l, wrh_ref[...]))
    rt_ref[...] = _route(z)


def _attn_out(o_p, o_s, x, mp4, ms3, layer, g_ffn, w_uv, w_o, w_rg, w_re, seq_len, tile):
    split_x = isinstance(x, tuple)
    n_p = o_p.shape[0]
    n_s = o_s.shape[0]
    din = o_p.shape[1]
    d = w_o.shape[1]
    n_pt, n_st = n_p // tile, n_s // tile
    tps = seq_len // tile
    n_batch = n_p // seq_len
    use_uv = w_uv is not None
    wr = jnp.concatenate([w_rg, w_re, jnp.zeros((d, LANES - N_GROUPS - N_EXPERTS), F32)], axis=-1)
    in_specs = _split_specs(n_pt, din, tile)
    args = [o_p, o_s]
    if split_x:
        in_specs += _split_specs(n_pt, d, tile)
        args += list(x)
    else:
        in_specs += [pl.BlockSpec((tile, d), lambda i: (i, 0))]
        args += [x]
    for k in (2, 3, 4):
        in_specs += _mod_specs(layer, k, n_pt, tps, n_batch, d, tile)
        args += [mp4, ms3]
    in_specs += [_full((1, d))]
    args += [g_ffn.reshape(1, d)]
    if use_uv:
        in_specs += [_full((MLA_HEADS, MLA_KV_RANK, MLA_V))]
        args += [jnp.transpose(w_uv, (1, 0, 2)).astype(BF16)]
    wr_hi = wr.astype(BF16)
    wr_lo = (wr - wr_hi.astype(F32)).astype(BF16)
    in_specs += [_full(w_o.shape), _full((d, LANES)), _full((d, LANES))]
    args += [w_o.astype(BF16), wr_hi, wr_lo]
    n_all = n_p + n_s
    tok = lambda cols: pl.BlockSpec((tile, cols), lambda i: (i, 0))
    return pl.pallas_call(
        functools.partial(_attn_out_kernel, n_pt, split_x, use_uv),
        grid=(n_pt + n_st,),
        in_specs=in_specs,
        out_specs=[tok(d), tok(d), tok(LANES)],
        out_shape=(jax.ShapeDtypeStruct((n_all, d), F32), jax.ShapeDtypeStruct((n_all, d), F32),
                   jax.ShapeDtypeStruct((n_all, LANES), F32)),
        compiler_params=_cparams(("arbitrary",)),
        name="attn_out_router",
    )(*args)


def _moe_kernel(tm, te_ref, tr_ref, src_ref, srcn_ref, dst_ref, w_ref, h_hbm, wg_ref, wu_ref, wd_ref,
                y_hbm, xbuf, ybuf, gsem, ssem):
    t = pl.program_id(0)
    nt = pl.num_programs(0)
    slot = t % 2
    rows = tr_ref[t]
    rows_next = tr_ref[jnp.minimum(t + 1, nt - 1)]
    rows_prev = tr_ref[jnp.maximum(t - 1, 0)]

    def gather_copy(idx_ref, r, sl):
        return pltpu.make_async_copy(h_hbm.at[pl.ds(idx_ref[0, 0, r], 1)], xbuf.at[sl, pl.ds(r, 1)], gsem.at[sl])

    def scatter_copy(r):
        return pltpu.make_async_copy(ybuf.at[pl.ds(r, 1)], y_hbm.at[pl.ds(dst_ref[0, 0, r], 1)], ssem.at[0])

    @pl.when(jnp.logical_and(t == 0, rows > 0))
    def _():
        for r in range(tm):
            gather_copy(src_ref, r, 0).start()

    @pl.when(jnp.logical_and(t + 1 < nt, rows_next > 0))
    def _():
        for r in range(tm):
            gather_copy(srcn_ref, r, 1 - slot).start()

    @pl.when(jnp.logical_and(t > 0, rows_prev > 0))
    def _():
        for r in range(tm):
            scatter_copy(r).wait()

    @pl.when(rows > 0)
    def _():
        for r in range(tm):
            gather_copy(src_ref, r, slot).wait()
        xb = xbuf[slot].astype(BF16)
        a = _dot(xb, wg_ref[...].astype(BF16))
        u = _dot(xb, wu_ref[...].astype(BF16))
        act = (a * _sigmoid(a)) * u * w_ref[...]
        ybuf[...] = _dot(act.astype(BF16), wd_ref[...].astype(BF16))
        for r in range(tm):
            scatter_copy(r).start(priority=r % 2)

    @pl.when(t == nt - 1)
    def _():
        @pl.when(rows > 0)
        def _():
            for r in range(tm):
                scatter_copy(r).wait()
        ybuf[...] = jnp.zeros(ybuf.shape, ybuf.dtype)
        fill = pltpu.make_async_copy(ybuf, y_hbm.at[pl.ds(y_hbm.shape[0] - tm, tm)], ssem.at[0])
        fill.start()
        fill.wait()


def _moe(h2, route, w_gate, w_up, w_down, layer, tm):
    n, d = h2.shape
    _, n_exp, _, f = w_gate.shape
    eid = route[:, :2].astype(I32)
    ew = route[:, 2:4]
    n2 = 2 * n
    flat = eid.reshape(n2)
    order = jnp.argsort(flat, stable=True).astype(I32)
    counts = jnp.sum((flat[:, None] == jnp.arange(n_exp, dtype=I32)[None, :]).astype(I32), axis=0)
    padded = ((counts + tm - 1) // tm) * tm
    pend = jnp.cumsum(padded)
    pstart = pend - padded
    ustart = jnp.cumsum(counts) - counts
    n_tiles = n2 // tm + n_exp
    tstart = jnp.arange(n_tiles, dtype=I32) * tm
    te = jnp.minimum(jnp.sum((pend[None, :] <= tstart[:, None]).astype(I32), axis=1), n_exp - 1)
    r0 = tstart - pstart[te]
    tr = jnp.clip(counts[te] - r0, 0, tm).astype(I32)
    lane = jnp.arange(tm, dtype=I32)[None, :]
    pos = (ustart[te] + r0)[:, None] + lane
    fidx = order[jnp.clip(pos, 0, n2 - 1)]
    tok = fidx >> 1
    src = tok.reshape(n_tiles, 1, tm)
    dst = jnp.where(lane < tr[:, None], (fidx & 1) * n + tok, n2 + lane).reshape(n_tiles, 1, tm)
    wslot = ew.reshape(n2)[fidx].reshape(n_tiles, tm, 1)

    smem_blk = lambda f_: pl.BlockSpec((1, 1, tm), f_, memory_space=pltpu.SMEM)
    wspec = lambda a_, b_: pl.BlockSpec((None, None, a_, b_), lambda t, te_, tr_: (layer, te_[t], 0, 0))
    grid_spec = pltpu.PrefetchScalarGridSpec(
        num_scalar_prefetch=2,
        grid=(n_tiles,),
        in_specs=[smem_blk(lambda t, te_, tr_: (t, 0, 0)),
                  smem_blk(lambda t, te_, tr_: (jnp.minimum(t + 1, n_tiles - 1), 0, 0)),
                  smem_blk(lambda t, te_, tr_: (t, 0, 0)),
                  pl.BlockSpec((None, tm, 1), lambda t, te_, tr_: (t, 0, 0)),
                  pl.BlockSpec(memory_space=pl.ANY),
                  wspec(d, f), wspec(d, f), wspec(f, d)],
        out_specs=pl.BlockSpec(memory_space=pl.ANY),
        scratch_shapes=[pltpu.VMEM((2, tm, d), F32), pltpu.VMEM((tm, d), F32),
                        pltpu.SemaphoreType.DMA((2,)), pltpu.SemaphoreType.DMA((1,))])
    return pl.pallas_call(
        functools.partial(_moe_kernel, tm),
        grid_spec=grid_spec,
        out_shape=jax.ShapeDtypeStruct((n2 + tm, d), F32),
        compiler_params=_cparams(("arbitrary",)),
        name="moe_experts",
    )(te, tr, src, src, dst, wslot, h2, w_gate, w_up, w_down)


def _combine_kernel(n_pt, final, d, *refs):
    if final:
        x_ref, y0_ref, y1_ref, gp_ref, gs_ref, gf_ref, op_ref, os_ref = refs
    else:
        x_ref, y0_ref, y1_ref, gp_ref, gs_ref, o_ref = refs
    i = pl.program_id(0)
    is_s = i >= n_pt
    gate = jnp.where(is_s, gs_ref[...], gp_ref[...])
    xn = x_ref[...] + gate * (y0_ref[...] + y1_ref[...])
    if final:
        out = _rms(xn, gf_ref[...])

        @pl.when(jnp.logical_not(is_s))
        def _():
            op_ref[...] = out

        @pl.when(is_s)
        def _():
            os_ref[...] = out
    else:
        o_ref[...] = xn


def _combine(x, y2, mp4, ms3, layer, g_final, n_p, seq_len, tile):
    n_all, d = x.shape
    n_s = n_all - n_p
    n_pt, n_st = n_p // tile, n_s // tile
    tps = seq_len // tile
    n_batch = n_p // seq_len
    final = g_final is not None
    tok = lambda cols: pl.BlockSpec((tile, cols), lambda i: (i, 0))
    n_tt = n_all // tile
    in_specs = ([tok(d), tok(d), pl.BlockSpec((tile, d), lambda i: (n_tt + i, 0))]
                + _mod_specs(layer, 5, n_pt, tps, n_batch, d, tile))
    args = [x, y2, y2, mp4, ms3]
    if final:
        in_specs += [_full((1, d))]
        args += [g_final.reshape(1, d)]
        out_specs = _split_specs(n_pt, d, tile)
        out_shape = (jax.ShapeDtypeStruct((n_p, d), F32), jax.ShapeDtypeStruct((n_s, d), F32))
    else:
        out_specs = tok(d)
        out_shape = jax.ShapeDtypeStruct((n_all, d), F32)
    return pl.pallas_call(
        functools.partial(_combine_kernel, n_pt, final, d),
        grid=(n_pt + n_st,),
        in_specs=in_specs, out_specs=out_specs, out_shape=out_shape,
        compiler_params=_cparams(("arbitrary",)),
        name="moe_combine",
    )(*args)


def _moba_qkv_kernel(n_pt, d, x_ref, shp_ref, shs_ref, scp_ref, scs_ref, gmix_ref, w_ref,
                     q_ref, kp_ref, ks_ref, vp_ref, vs_ref, kb_ref, vb_ref, ksum_ref):
    i = pl.program_id(0)
    is_s = i >= n_pt
    shift = jnp.where(is_s, shs_ref[...], shp_ref[...])
    scale = jnp.where(is_s, scs_ref[...], scp_ref[...])
    h = (_rms(x_ref[...], gmix_ref[...]) * (1.0 + scale) + shift).astype(BF16)
    qkv = _dot(h, w_ref[...])
    q_ref[...] = qkv[:, :d]

    @pl.when(jnp.logical_not(is_s))
    def _():
        k = qkv[:, d:2 * d]
        v = qkv[:, 2 * d:]
        kp_ref[...] = k
        vp_ref[...] = v
        vb_ref[...] = v.astype(BF16)
        nb = k.shape[0] // MOBA_BLOCK
        for n in range(nb):
            kb_ref[n] = jnp.transpose(k[n * MOBA_BLOCK:(n + 1) * MOBA_BLOCK, :]).astype(BF16)
        means = [jnp.sum(k[n * MOBA_BLOCK:(n + 1) * MOBA_BLOCK, :], axis=0, keepdims=True) * (1.0 / MOBA_BLOCK)
                 for n in range(nb)]
        ksum_ref[...] = jnp.concatenate(means + [jnp.zeros((8 - nb, d), F32)], axis=0)

    @pl.when(is_s)
    def _():
        ks_ref[...] = qkv[:, d:2 * d]
        vs_ref[...] = qkv[:, 2 * d:]


def _moba_qkv(x, mp4, ms3, layer, g_mix, w_qkv, n_p, seq_len, tile):
    n_all, d = x.shape
    n_s = n_all - n_p
    n_pt, n_st = n_p // tile, n_s // tile
    tps = seq_len // tile
    n_batch = n_p // seq_len
    tok = lambda cols: pl.BlockSpec((tile, cols), lambda i: (i, 0))
    in_specs = ([tok(d)] + _mod_specs(layer, 0, n_pt, tps, n_batch, d, tile)
                + _mod_specs(layer, 1, n_pt, tps, n_batch, d, tile) + [_full((1, d)), _full((d, 3 * d))])
    sp = _split_specs(n_pt, d, tile)
    assert tile % MOBA_BLOCK == 0 and tile // MOBA_BLOCK <= 8
    nb = tile // MOBA_BLOCK
    last = lambda i: jnp.minimum(i, n_pt - 1)
    outs = pl.pallas_call(
        functools.partial(_moba_qkv_kernel, n_pt, d),
        grid=(n_pt + n_st,),
        in_specs=in_specs,
        out_specs=[tok(d), sp[0], sp[1], sp[0], sp[1],
                   pl.BlockSpec((None, nb, d, MOBA_BLOCK), lambda i: (last(i) // tps, last(i) % tps, 0, 0)),
                   sp[0],
                   pl.BlockSpec((8, d), lambda i: (last(i), 0))],
        out_shape=(jax.ShapeDtypeStruct((n_all, d), F32),
                   jax.ShapeDtypeStruct((n_p, d), F32), jax.ShapeDtypeStruct((n_s, d), F32),
                   jax.ShapeDtypeStruct((n_p, d), F32), jax.ShapeDtypeStruct((n_s, d), F32),
                   jax.ShapeDtypeStruct((n_batch, seq_len // MOBA_BLOCK, d, MOBA_BLOCK), BF16),
                   jax.ShapeDtypeStruct((n_p, d), BF16),
                   jax.ShapeDtypeStruct((n_pt * 8, d), F32)),
        compiler_params=_cparams(("arbitrary",)),
        name="moba_qkv",
    )(x, mp4, ms3, mp4, ms3, g_mix.reshape(1, d), w_qkv.astype(BF16))
    q_all, k_p, k_s, v_p, v_s, k_b, v_b, ksum = outs
    kmean = ksum.reshape(n_pt, 8, d)[:, :nb, :].reshape(n_pt * nb, d)
    return q_all, k_p, k_s, v_p, v_s, k_b, v_b, kmean


def _t5_bias(dist, table_at):
    val = jnp.full(dist.shape, 0.0, F32) + table_at(0)
    for b in range(1, REL_BUCKETS):
        val = jnp.where(dist >= T5_THR[b], table_at(b), val)
    return val


MOBA_DROP = -32768.0


def _moba_prompt_kernel(q_ref, kt_ref, v_ref, mean_ref, tab_ref, o_ref,
                        bown_ref, bprev_ref, qa_ref, acc_ref, m_ref, l_ref):
    b = pl.program_id(0)
    i = pl.program_id(1)
    nh, dh, blk = MOBA_HEADS, MOBA_HEAD_DIM, MOBA_BLOCK
    nblk = kt_ref.shape[0]
    scale = dh ** -0.5
    ones = jnp.ones((blk, LANES), BF16)
    rep = lambda x: jnp.concatenate([x] * (blk // LANES), axis=-1)
    rowmax = lambda s: jnp.broadcast_to(jnp.max(s, axis=-1, keepdims=True), (blk, LANES))
    r = lax.broadcasted_iota(I32, (blk, blk), 0)
    c = lax.broadcasted_iota(I32, (blk, blk), 1)

    @pl.when(jnp.logical_and(b == 0, i == 0))
    def _():
        for h in range(nh):
            at = lambda bb, h=h: tab_ref[bb, h]
            bown_ref[h] = _t5_bias(jnp.maximum(r - c, 0), at)
            bprev_ref[h] = _t5_bias(blk + r - c, at)

    off = pl.multiple_of(i * blk, blk)
    bi = lax.broadcasted_iota(I32, (nblk, blk), 0)
    valid = bi < i
    for h in range(nh):
        hs = slice(h * dh, (h + 1) * dh)
        q = q_ref[:, hs]
        qb = q.astype(BF16)
        gate_t = jnp.where(valid, _dot_nt_f32(mean_ref[:, hs], q), NEG)
        rank = jnp.zeros(gate_t.shape, I32)
        for m_ in range(nblk - 1):
            gm = gate_t[m_:m_ + 1, :]
            ahead = jnp.logical_or(gm > gate_t, jnp.logical_and(gm == gate_t, m_ < bi))
            rank = rank + jnp.where(jnp.logical_and(ahead, m_ < i), 1, 0)
        drop_t = jnp.where(jnp.logical_and(valid, rank < MOBA_TOPK), 0.0, MOBA_DROP)
        drop = jnp.transpose(jnp.concatenate([drop_t, jnp.zeros((LANES - nblk, blk), F32)], axis=0))
        qa_ref[h] = jnp.concatenate([qb, drop.astype(BF16)], axis=-1)

        vo = jnp.concatenate([v_ref[pl.ds(off, blk), hs], ones], axis=-1)
        s = _dot(qb, kt_ref[i, hs, :]) * scale + bown_ref[h]
        s = jnp.where(c <= r, s, NEG)
        m = rowmax(s)
        p = jnp.exp(s - rep(m))
        pv = _dot(p.astype(BF16), vo)
        m_ref[h] = m
        l_ref[h] = pv[:, dh:]
        acc_ref[h] = pv[:, :dh]

    srow = lax.broadcasted_iota(I32, (LANES, blk), 0)

    def body(j, carry):
        offj = pl.multiple_of(j * blk, blk)
        onehot_t = jnp.where(srow == j, 1.0, 0.0).astype(BF16)
        is_prev = j == i - 1
        for h in range(nh):
            hs = slice(h * dh, (h + 1) * dh)
            kj = jnp.concatenate([kt_ref[j, hs, :], onehot_t], axis=0)
            vj = jnp.concatenate([v_ref[pl.ds(offj, blk), hs], ones], axis=-1)
            bias = jnp.where(is_prev, bprev_ref[h], tab_ref[REL_BUCKETS - 1, h])
            s = _dot(qa_ref[h], kj) * scale + bias
            m = m_ref[h]
            m_new = jnp.maximum(m, rowmax(s))
            alpha = jnp.exp(m - m_new)
            p = jnp.exp(s - rep(m_new))
            pv = _dot(p.astype(BF16), vj)
            m_ref[h] = m_new
            l_ref[h] = alpha * l_ref[h] + pv[:, dh:]
            acc_ref[h] = acc_ref[h] * alpha + pv[:, :dh]
        return carry

    lax.fori_loop(0, i, body, 0)
    for h in range(nh):
        o_ref[:, h * dh:(h + 1) * dh] = (acc_ref[h] / l_ref[h]).astype(o_ref.dtype)


def _moba_prompt_attn(q_all, k_b, v_b, kmean, rel_table, n_batch, seq_len):
    blk = MOBA_BLOCK
    nq = seq_len // blk
    dh = MOBA_HEAD_DIM
    nh = MOBA_HEADS
    d = nh * dh
    assert nq <= LANES and (nq % 8 == 0 or n_batch == 1) and dh == LANES
    return pl.pallas_call(
        _moba_prompt_kernel,
        grid=(n_batch, nq),
        in_specs=[pl.BlockSpec((blk, d), lambda b, i: (b * nq + i, 0)),
                  pl.BlockSpec((None, nq, d, blk), lambda b, i: (b, 0, 0, 0)),
                  pl.BlockSpec((seq_len, d), lambda b, i: (b, 0)),
                  pl.BlockSpec((nq, d), lambda b, i: (b, 0)),
                  pl.BlockSpec(memory_space=pltpu.SMEM)],
        out_specs=pl.BlockSpec((blk, d), lambda b, i: (b * nq + i, 0)),
        out_shape=jax.ShapeDtypeStruct((n_batch * seq_len, d), BF16),
        scratch_shapes=[pltpu.VMEM((nh, blk, blk), F32), pltpu.VMEM((nh, blk, blk), F32),
                        pltpu.VMEM((nh, blk, dh + LANES), BF16), pltpu.VMEM((nh, blk, dh), F32),
                        pltpu.VMEM((nh, blk, LANES), F32), pltpu.VMEM((nh, blk, LANES), F32)],
        compiler_params=_cparams(("arbitrary", "arbitrary")),
        name="moba_prompt_attn",
    )(q_all, k_b, v_b, kmean, rel_table)


def _moba_sample_kernel(layer, n_pages, page, t_new, cpages,
                        pt_ref, q_ref, kn_ref, vn_ref, trow_ref, k_hbm, v_hbm, o_ref,
                        buf, sum_ref, s_buf, sem):
    b = pl.program_id(0)
    nb = pl.num_programs(0)
    nh, dh, blk = MOBA_HEADS, MOBA_HEAD_DIM, MOBA_BLOCK
    rows = nh * t_new
    past = n_pages * page
    chunk = cpages * page
    n_chunks = n_pages // cpages
    nblk = past // blk
    bpc = chunk // blk
    scale = dh ** -0.5

    def start_chunk(seq, ci, src, sl):
        def body(j, carry):
            pid = pt_ref[seq, ci * cpages + j]
            dst_rows = pl.ds(pl.multiple_of(j * page, page), page)
            for h in range(nh):
                pltpu.make_async_copy(src.at[layer, pid, :, h, :], buf.at[sl, h, dst_rows, :], sem.at[sl]).start()
            return carry
        lax.fori_loop(0, cpages, body, 0)

    def wait_chunk(src, sl):
        for _ in range(cpages * nh):
            pltpu.make_async_copy(src.at[layer, 0, :, 0, :], buf.at[sl, 0, pl.ds(0, page), :], sem.at[sl]).wait()

    @pl.when(b == 0)
    def _():
        for ci in range(min(2, n_chunks)):
            start_chunk(0, ci, k_hbm, ci)
        buf[2] = jnp.zeros(buf.shape[1:], buf.dtype)
        buf[3] = jnp.zeros(buf.shape[1:], buf.dtype)

    q = q_ref[...]
    qh = [q[:, h * dh:(h + 1) * dh] for h in range(nh)]
    qh_b = [x.astype(BF16) for x in qh]

    for ci in range(n_chunks):
        sl = ci % 2
        wait_chunk(k_hbm, sl)
        for h in range(nh):
            kf = buf[sl, h]
            sum_ref[h, ci * bpc:(ci + 1) * bpc, :] = jnp.sum(kf.reshape(bpc, blk, dh), axis=1)
            s_buf[h * t_new:(h + 1) * t_new, ci * chunk:(ci + 1) * chunk] = _dot_nt(qh_b[h], kf.astype(BF16)) * scale
        if ci + 2 < n_chunks:
            start_chunk(b, ci + 2, k_hbm, sl)

    @pl.when(b + 1 < nb)
    def _():
        for ci in range(min(2, n_chunks)):
            start_chunk(b + 1, ci, k_hbm, ci)

    gate = jnp.concatenate(
        [_dot_nt_f32(qh[h], sum_ref[h] * (1.0 / blk)) for h in range(nh)], axis=0)
    bl = lax.broadcasted_iota(I32, gate.shape, 1)
    big = jnp.int32(1 << 20)
    sel = jnp.zeros(gate.shape, F32)
    g = gate
    for _ in range(min(MOBA_TOPK, nblk)):
        gm = jnp.max(g, axis=-1, keepdims=True)
        gi = jnp.min(jnp.where(g == gm, bl, big), axis=-1, keepdims=True)
        hit = bl == gi
        sel = jnp.where(hit, 1.0, sel)
        g = jnp.where(hit, NEG * 2.0, g)

    lane1 = lax.broadcasted_iota(I32, (1, nblk), 1)
    pw = jnp.left_shift(1, lane1 & 15).astype(F32)
    words = []
    for h in range(nh):
        need_h = jnp.max(sel[h * t_new:(h + 1) * t_new, :], axis=0, keepdims=True) * pw
        words.append([jnp.sum(jnp.where((lane1 >> 4) == w, need_h, 0.0)).astype(I32)
                      for w in range((nblk + 15) // 16)])
    ppb = blk // page

    def v_copies(ci, sl, n, h):
        out = []
        for pg in range(ppb):
            j = n * ppb + pg
            pid = pt_ref[b, ci * cpages + j]
            out.append(pltpu.make_async_copy(v_hbm.at[layer, pid, :, h, :],
                                             buf.at[sl, h, pl.ds(j * page, page), :], sem.at[sl]))
        return out

    def for_needed(ci, fn):
        for n in range(bpc):
            g_blk = ci * bpc + n
            for h in range(nh):
                @pl.when(((words[h][g_blk // 16] >> (g_blk % 16)) & 1) == 1)
                def _(n=n, h=h):
                    fn(n, h)

    def start_v(ci, sl):
        for_needed(ci, lambda n, h: [cp.start() for cp in v_copies(ci, sl, n, h)])

    def wait_v(ci, sl):
        for_needed(ci, lambda n, h: [cp.wait() for cp in v_copies(ci, sl, n, h)])

    for ci in range(min(2, n_chunks)):
        start_v(ci, 2 + ci)
    kpos_blk = lax.broadcasted_iota(I32, (nblk, past), 1) // blk
    expand = jnp.where(kpos_blk == lax.broadcasted_iota(I32, (nblk, past), 0), 1.0, 0.0).astype(BF16)
    keep = _dot(sel.astype(BF16), expand)

    trow = trow_ref[...]
    col = lambda b_: trow[:, b_:b_ + 1]
    t_of_row = lax.broadcasted_iota(I32, (rows, LANES), 0) & (t_new - 1)
    kpos_tail = (past - LANES) + lax.broadcasted_iota(I32, (rows, LANES), 1)
    bias_tail = _t5_bias(past + t_of_row - kpos_tail, col)
    s = s_buf[...] + col(REL_BUCKETS - 1)
    s_tail = s_buf[:, past - LANES:] + bias_tail
    s = jnp.concatenate([s[:, :past - LANES], s_tail], axis=-1)
    s = jnp.where(keep > 0.5, s, NEG)

    kn = kn_ref[...].astype(BF16)
    vn = vn_ref[...].astype(BF16)
    s_own = jnp.concatenate([_dot_nt(qh_b[h], kn[:, h * dh:(h + 1) * dh]) for h in range(nh)], axis=0) * scale
    r_t = lax.broadcasted_iota(I32, s_own.shape, 0) & (t_new - 1)
    c_t = lax.broadcasted_iota(I32, s_own.shape, 1)
    s_own = s_own + _t5_bias(jnp.maximum(r_t - c_t, 0), col)
    s_own = jnp.where(c_t <= r_t, s_own, NEG)

    m = jnp.maximum(jnp.max(s, axis=-1, keepdims=True), jnp.max(s_own, axis=-1, keepdims=True))
    p = jnp.exp(s - m)
    p_own = jnp.exp(s_own - m)
    l = jnp.sum(p, axis=-1, keepdims=True) + jnp.sum(p_own, axis=-1, keepdims=True)
    s_buf[...] = p
    p_own = p_own.astype(BF16)
    acc = [_dot(p_own[h * t_new:(h + 1) * t_new, :], vn[:, h * dh:(h + 1) * dh]) for h in range(nh)]

    for ci in range(n_chunks):
        sl = 2 + ci % 2
        wait_v(ci, sl)
        for h in range(nh):
            vh = buf[sl, h].astype(BF16)
            ph = s_buf[h * t_new:(h + 1) * t_new, ci * chunk:(ci + 1) * chunk].astype(BF16)
            acc[h] = acc[h] + _dot(ph, vh)
        if ci + 2 < n_chunks:
            start_v(ci + 2, sl)
    for h in range(nh):
        o_ref[:, h * dh:(h + 1) * dh] = acc[h] / l[h * t_new:(h + 1) * t_new, :]


def _moba_sample_attn(page_table, q_all, k_s, v_s, rel_table, cache_k, cache_v, layer, n_p, t_new):
    n_seq, n_pages = page_table.shape
    n_layers, n_phys, page, nh, dh = cache_k.shape
    d = nh * dh
    past = n_pages * page
    cpages = min(8, n_pages // 2)
    assert n_pages % cpages == 0 and (cpages * page) % MOBA_BLOCK == 0
    rows = nh * t_new
    q_off = n_p // t_new
    trow = jnp.repeat(rel_table.T, t_new, axis=0)
    grid_spec = pltpu.PrefetchScalarGridSpec(
        num_scalar_prefetch=1,
        grid=(n_seq,),
        in_specs=[pl.BlockSpec((t_new, d), lambda b, pt: (q_off + b, 0)),
                  pl.BlockSpec((t_new, d), lambda b, pt: (b, 0)),
                  pl.BlockSpec((t_new, d), lambda b, pt: (b, 0)),
                  pl.BlockSpec((rows, REL_BUCKETS), lambda b, pt: (0, 0)),
                  pl.BlockSpec(memory_space=pl.ANY),
                  pl.BlockSpec(memory_space=pl.ANY)],
        out_specs=pl.BlockSpec((t_new, d), lambda b, pt: (b, 0)),
        scratch_shapes=[pltpu.VMEM((4, nh, cpages * page, dh), F32),
                        pltpu.VMEM((nh, past // MOBA_BLOCK, dh), F32),
                        pltpu.VMEM((rows, past), F32),
                        pltpu.SemaphoreType.DMA((4,))])
    return pl.pallas_call(
        functools.partial(_moba_sample_kernel, layer, n_pages, page, t_new, cpages),
        grid_spec=grid_spec,
        out_shape=jax.ShapeDtypeStruct((n_seq * t_new, d), F32),
        compiler_params=_cparams(("arbitrary",)),
        name="moba_sample_attn",
    )(page_table, q_all, k_s, v_s, trow, cache_k, cache_v)


def _rope_tables(pos):
    half = MLA_ROPE // 2
    inv_freq = ROPE_THETA ** (-jnp.arange(half, dtype=F32) / half)
    ang = pos.astype(F32)[:, None] * inv_freq
    z = jnp.zeros((pos.shape[0], LANES - MLA_ROPE), F32)
    cos = jnp.concatenate([jnp.cos(ang), jnp.cos(ang), z], axis=-1)
    sin = jnp.concatenate([jnp.sin(ang), jnp.sin(ang), z], axis=-1)
    return cos, sin


def kernel(x_prompt, x_sample, cache_mla_ckv, cache_mla_kpe, cache_moba_k, cache_moba_v, page_table, c_prompt, c_sample, w_ada, b_ada, g_norm_mix, g_norm_ffn, g_final, w_dq, g_q, w_uq, w_dkv, g_kv, w_uk, w_uv, w_o_mla, w_qkv_moba, w_o_moba, rel_bias_table, w_router_group, w_router_expert, w_exp_gate, w_exp_up, w_exp_down):
    n_batch, seq_len, d = x_prompt.shape
    n_seq, t_new, _ = x_sample.shape
    depth = w_ada.shape[0]
    page = cache_mla_ckv.shape[2]
    past = page_table.shape[1] * page
    n_p, n_s = n_batch * seq_len, n_seq * t_new
    tile = min(TOK_TILE, n_s)
    assert n_s % tile == 0 and seq_len % tile == 0 and tile % t_new == 0

    xp = x_prompt.reshape(n_p, d)
    xs = x_sample.reshape(n_s, d)
    m = _ada(jnp.concatenate([c_sample, c_prompt], axis=0), w_ada, b_ada)
    mp4 = m[:, n_seq:, :].reshape(depth, n_batch, 1, 6 * d)
    ms3 = jnp.repeat(m[:, :n_seq, :], t_new, axis=1)

    cos_p, sin_p = _rope_tables(jnp.arange(seq_len))
    cos_s, sin_s = _rope_tables(past + (jnp.arange(tile) % t_new))
    cos_all = jnp.concatenate([cos_p, cos_s], axis=0)
    sin_all = jnp.concatenate([sin_p, sin_s], axis=0)

    outs = {}
    x = (xp, xs)
    for i in range(depth):
        if i % 2 == 0:
            la = i // 2
            if not isinstance(x, tuple):
                x = (x[:n_p], x[n_p:])
            ckv_p, ckv_s, kpe_p, kpe_s, q_all, k_all, q_s, k_s = _mla_proj(
                x[0], x[1], mp4, ms3, i, g_norm_mix[i], cos_all, sin_all,
                w_dq[la], g_q[la], w_uq[la], w_dkv[la], g_kv[la], w_uk[la], seq_len, tile)
            o_p = _mla_prompt_attn(q_all, k_all, n_batch, seq_len, min(ATT_TILE, seq_len))
            o_s = _mla_sample_attn(page_table, q_s, k_s, cache_mla_ckv, cache_mla_kpe, la, t_new)
            outs.setdefault("ckv_p", []).append(ckv_p.reshape(n_batch, seq_len, MLA_KV_RANK))
            outs.setdefault("kpe_p", []).append(kpe_p.reshape(n_batch, seq_len, MLA_ROPE))
            outs.setdefault("ckv_s", []).append(ckv_s.reshape(n_seq, t_new, MLA_KV_RANK))
            outs.setdefault("kpe_s", []).append(kpe_s.reshape(n_seq, t_new, MLA_ROPE))
            x_new, h2, route = _attn_out(o_p, o_s, x, mp4, ms3, i, g_norm_ffn[i], w_uv[la], w_o_mla[la],
                                         w_router_group[i], w_router_expert[i], seq_len, tile)
        else:
            lb = i // 2
            xu = x if not isinstance(x, tuple) else jnp.concatenate(x, axis=0)
            q_all, k_p, k_s, v_p, v_s, k_b, v_b, kmean = _moba_qkv(
                xu, mp4, ms3, i, g_norm_mix[i], w_qkv_moba[lb], n_p, seq_len, tile)
            o_p = _moba_prompt_attn(q_all, k_b, v_b, kmean, rel_bias_table, n_batch, seq_len)
            o_s = _moba_sample_attn(page_table, q_all, k_s, v_s, rel_bias_table, cache_moba_k, cache_moba_v,
                                    lb, n_p, t_new)
            hd = (MOBA_HEADS, MOBA_HEAD_DIM)
            outs.setdefault("k_p", []).append(k_p.reshape(n_batch, seq_len, *hd))
            outs.setdefault("v_p", []).append(v_p.reshape(n_batch, seq_len, *hd))
            outs.setdefault("k_s", []).append(k_s.reshape(n_seq, t_new, *hd))
            outs.setdefault("v_s", []).append(v_s.reshape(n_seq, t_new, *hd))
            x_new, h2, route = _attn_out(o_p, o_s, xu, mp4, ms3, i, g_norm_ffn[i], None, w_o_moba[lb],
                                         w_router_group[i], w_router_expert[i], seq_len, tile)
        y2 = _moe(h2, route, w_exp_gate, w_exp_up, w_exp_down, i, MOE_TILE)
        last = i == depth - 1
        x = _combine(x_new, y2, mp4, ms3, i, g_final if last else None, n_p, seq_len, tile)
    y_p, y_s = x
    st = lambda key: jnp.stack(outs[key])
    return (y_p.reshape(n_batch, seq_len, d), y_s.reshape(n_seq, t_new, d),
            st("ckv_p"), st("kpe_p"), st("k_p"), st("v_p"), st("ckv_s"), st("kpe_s"), st("k_s"), st("v_s"))
```
